```python
import math
import jax, jax.numpy as jnp
from jax import lax
import numpy as np

D_MODEL = 1024
BATCH = 4
SEQ = 4096
DEPTH = 2
DEC_BATCH = 32
DEC_SEQ = 1
PAST_LEN = 8192
PAGE_SIZE = 128

A_HEADS = 8
A_HEAD_DIM = 64
A_WIDTH = A_HEADS * A_HEAD_DIM
MOBA_BLOCK = 256
MOBA_TOPK = 3
MOBA_Q_CHUNK = 16
R_HEADS = 8
R_HEAD_DIM = 64
R_WIDTH = R_HEADS * R_HEAD_DIM
LORA_DECAY = 64
LORA_ICLR = 64
LORA_GATE = 160
R_PROJ = 3 * R_WIDTH + LORA_DECAY + LORA_ICLR + LORA_GATE
GN_EPS = 64e-5
PROJ_WIDTH = 3 * A_WIDTH + R_PROJ + 2 * D_MODEL
D_FF = 2816
N_EXPERTS = 8
MOE_TOPK = 2
D_FF_EXPERT = 1408
MOE_ROW_BLOCK = 128
N_DENSE = (DEPTH + 1) // 2
N_MOE = DEPTH // 2
RMS_EPS = 1e-6

kernel_name = 'moba_rwkv7_gated_hybrid_step'


def rms_norm(x, gain):
    xf = x.astype(jnp.float32)
    y = xf * lax.rsqrt(jnp.mean(xf * xf, axis=-1, keepdims=True) + RMS_EPS)
    return (y * gain).astype(x.dtype)


def moba_attention(q, k, v, q_pos):
    B, Q, H, hd = q.shape
    T = k.shape[1]
    n_blocks = -(-T // MOBA_BLOCK)
    pad = n_blocks * MOBA_BLOCK - T
    kb = jnp.pad(k, ((0, 0), (0, pad), (0, 0), (0, 0))).reshape(B, n_blocks, MOBA_BLOCK, H, hd)
    vb = jnp.pad(v, ((0, 0), (0, pad), (0, 0), (0, 0))).reshape(B, n_blocks, MOBA_BLOCK, H, hd)
    k_mean = jnp.mean(kb.astype(jnp.float32), axis=2)
    n_sel = min(MOBA_TOPK, n_blocks)
    blk_ids = jnp.arange(n_blocks, dtype=jnp.int32)
    offs = jnp.arange(MOBA_BLOCK, dtype=jnp.int32)
    bi = jnp.arange(B, dtype=jnp.int32)[:, None, None, None]
    hi = jnp.arange(H, dtype=jnp.int32)[None, None, :, None]
    scale = hd ** -0.5
    q_chunk = math.gcd(Q, MOBA_Q_CHUNK)

    def attend(args):
        qc, pc = args
        C = qc.shape[1]
        own = pc // MOBA_BLOCK
        gate = jnp.einsum('bchd,bnhd->bchn', qc.astype(jnp.float32), k_mean)
        fully_past = blk_ids[None, :] < own[:, None]
        gate = jnp.where(fully_past[None, :, None, :], gate, -jnp.inf)
        _, top = lax.top_k(gate, n_sel)
        own_b = jnp.broadcast_to(own[None, :, None, None], (B, C, H, 1)).astype(top.dtype)
        sel = jnp.concatenate([top, own_b], axis=-1)
        sel_ok = jnp.concatenate([top < own_b, jnp.ones((B, C, H, 1), dtype=bool)], axis=-1)
        ks = kb[bi, sel, :, hi]
        vs = vb[bi, sel, :, hi]
        key_pos = sel[..., None] * MOBA_BLOCK + offs
        mask = sel_ok[..., None] & (key_pos <= pc[None, :, None, None, None])
        s = jnp.einsum('bchd,bchskd->bchsk', qc, ks).astype(jnp.float32) * scale
        s = jnp.where(mask, s, -jnp.inf)
        p = jax.nn.softmax(s.reshape(B, C, H, -1), axis=-1).reshape(s.shape)
        return jnp.einsum('bchsk,bchskd->bchd', p.astype(vs.dtype), vs)

    n_chunks = Q // q_chunk
    qs = q.reshape(B, n_chunks, q_chunk, H, hd).transpose(1, 0, 2, 3, 4)
    ps = q_pos.reshape(n_chunks, q_chunk)
    o = lax.map(attend, (qs, ps))
    return o.transpose(1, 0, 2, 3, 4).reshape(B, Q, H, hd)


def rwkv7_time_mix(p, p_prev, s0, mu, w0, w_dec2, a0, w_a2, w_g2, k_k, k_a, r_k, ln_w, ln_b):
    B, L, _ = p.shape
    f32 = jnp.float32
    p_shift = jnp.concatenate([p_prev[:, None, :].astype(p.dtype), p[:, :-1]], axis=1)
    pm = p + (p_shift - p) * mu
    cuts = [R_WIDTH, 2 * R_WIDTH, 3 * R_WIDTH, 3 * R_WIDTH + LORA_DECAY, 3 * R_WIDTH + LORA_DECAY + LORA_ICLR]
    r, k, v, wl, al, gl = jnp.split(pm, cuts, axis=-1)
    w_log = -jax.nn.softplus(-(w0 + jnp.tanh(wl) @ w_dec2).astype(f32)) - 0.5
    decay = jnp.exp(-jnp.exp(w_log))
    a = jax.nn.sigmoid((a0 + al @ w_a2).astype(f32))
    g = jax.nn.sigmoid(gl) @ w_g2
    heads = lambda t: t.reshape(B, L, R_HEADS, R_HEAD_DIM)
    kk = heads((k * k_k).astype(f32))
    kk = kk / jnp.maximum(jnp.linalg.norm(kk, axis=-1, keepdims=True), 1e-12)
    k_mod = k.astype(f32) * (1.0 + (a - 1.0) * k_a)
    rh, kh, vh = heads(r.astype(f32)), heads(k_mod), heads(v.astype(f32))
    wh, ah = heads(decay), heads(a)
    bh = kk * ah

    def step(S, xs):
        r_t, k_t, v_t, w_t, kk_t, b_t = xs
        sa = jnp.einsum('bhij,bhj->bhi', S, kk_t)
        S = S * w_t[:, :, None, :] - sa[..., None] * b_t[:, :, None, :] + v_t[..., None] * k_t[:, :, None, :]
        return S, jnp.einsum('bhij,bhj->bhi', S, r_t)

    tm = lambda t: jnp.moveaxis(t, 1, 0)
    s_final, y = lax.scan(step, s0.astype(f32), (tm(rh), tm(kh), tm(vh), tm(wh), tm(kk), tm(bh)))
    y = jnp.moveaxis(y, 0, 1)
    mean = jnp.mean(y, axis=-1, keepdims=True)
    var = jnp.mean(jnp.square(y - mean), axis=-1, keepdims=True)
    yn = ((y - mean) * lax.rsqrt(var + GN_EPS)).reshape(B, L, R_WIDTH) * ln_w + ln_b
    bonus = (jnp.sum(rh * kh * r_k, axis=-1, keepdims=True) * vh).reshape(B, L, R_WIDTH)
    out = ((yn + bonus) * g).astype(p.dtype)
    return out, s_final, p[:, -1]


def hybrid_mixer(x, k_past, v_past, s0, p_prev, norm_a, w_in, w_up_a, mu, w0, w_dec2, a0, w_a2, w_g2,
                 k_k, k_a, r_k, ln_w, ln_b, w_up_b, w_out):
    B, L, _ = x.shape
    past_len = k_past.shape[1]
    h = rms_norm(x, norm_a)
    proj = h @ w_in
    qa, ka, va, pr, gates = jnp.split(proj, [A_WIDTH, 2 * A_WIDTH, 3 * A_WIDTH, 3 * A_WIDTH + R_PROJ], axis=-1)
    ha = lambda t: t.reshape(B, L, A_HEADS, A_HEAD_DIM)
    qa, ka, va = ha(qa), ha(ka), ha(va)
    keys = jnp.concatenate([k_past.astype(ka.dtype), ka], axis=1)
    values = jnp.concatenate([v_past.astype(va.dtype), va], axis=1)
    q_pos = past_len + jnp.arange(L, dtype=jnp.int32)
    oa = moba_attention(qa, keys, values, q_pos)
    ya = oa.reshape(B, L, A_WIDTH) @ w_up_a
    ob, s_new, p_last = rwkv7_time_mix(pr, p_prev, s0, mu, w0, w_dec2, a0, w_a2, w_g2, k_k, k_a, r_k, ln_w, ln_b)
    yb = ob @ w_up_b
    g = jax.nn.sigmoid(gates.astype(jnp.float32)).astype(x.dtype)
    merged = g[..., :D_MODEL] * ya + g[..., D_MODEL:] * yb
    return x + merged @ w_out, ka, va, s_new, p_last


def swiglu(x, w1, w3, w2):
    return (jax.nn.silu(x @ w1) * (x @ w3)) @ w2


def moe_swiglu(x, w_router, w1, w3, w2):
    B, L, D = x.shape
    xt = x.reshape(-1, D)
    N = xt.shape[0]
    logits = (xt @ w_router).astype(jnp.float32)
    top_logit, top_e = lax.top_k(logits, MOE_TOPK)
    gate = jax.nn.softmax(top_logit, axis=-1)
    A = N * MOE_TOPK
    e_flat = top_e.reshape(A)
    tok_flat = jnp.arange(A, dtype=jnp.int32) // MOE_TOPK
    g_flat = gate.reshape(A)
    order = jnp.argsort(e_flat)
    e_sorted = e_flat[order]
    counts = jnp.zeros((N_EXPERTS,), jnp.int32).at[e_flat].add(1)
    padded = (counts + MOE_ROW_BLOCK - 1) // MOE_ROW_BLOCK * MOE_ROW_BLOCK
    start = jnp.cumsum(counts) - counts
    pend = jnp.cumsum(padded)
    pstart = pend - padded
    dest = pstart[e_sorted] + jnp.arange(A, dtype=jnp.int32) - start[e_sorted]
    R = -(-(A + N_EXPERTS * (MOE_ROW_BLOCK - 1)) // MOE_ROW_BLOCK) * MOE_ROW_BLOCK
    row_tok = jnp.full((R,), N, dtype=jnp.int32).at[dest].set(tok_flat[order])
    row_gate = jnp.zeros((R,), jnp.float32).at[dest].set(g_flat[order])
    n_blk = R // MOE_ROW_BLOCK
    blk_start = jnp.arange(n_blk, dtype=jnp.int32) * MOE_ROW_BLOCK
    blk_e = jnp.minimum(jnp.searchsorted(pend, blk_start, side='right'), N_EXPERTS - 1)
    x_pad = jnp.concatenate([xt, jnp.zeros((1, D), xt.dtype)], axis=0)
    xb = x_pad[row_tok].reshape(n_blk, MOE_ROW_BLOCK, D)

    def expert_block(args):
        xblk, e = args
        return swiglu(xblk, w1[e], w3[e], w2[e])

    yb = lax.map(expert_block, (xb, blk_e)).reshape(R, D)
    out = jnp.zeros((N + 1, D), yb.dtype).at[row_tok].add(yb * row_gate[:, None].astype(yb.dtype))
    return out[:N].reshape(B, L, D)


def channel_mixer(x, l, norm_ffn, ffn_w1, ffn_w3, ffn_w2, router, moe_w1, moe_w3, moe_w2):
    h = rms_norm(x, norm_ffn[l])
    i = l // 2
    if l % 2 == 0:
        f = swiglu(h, ffn_w1[i], ffn_w3[i], ffn_w2[i])
    else:
        f = moe_swiglu(h, router[i], moe_w1[i], moe_w3[i], moe_w2[i])
    return x + f


def setup_inputs(seed: int = 0) -> dict:
    key = jax.random.key(seed)
    ks = jax.random.split(key, 40)
    f32 = jnp.float32
    n_pages = PAST_LEN // PAGE_SIZE
    n_pool = (DEC_BATCH * n_pages * 5) // 4
    nrm = lambda k, shape, s: jax.random.normal(k, shape, f32) * s
    uni = lambda k, shape, lo, hi: jax.random.uniform(k, shape, f32, minval=lo, maxval=hi)
    x_prompt = nrm(ks[0], (BATCH, SEQ, D_MODEL), 1.0)
    x_sample = nrm(ks[1], (DEC_BATCH, DEC_SEQ, D_MODEL), 1.0)
    cache_k = nrm(ks[2], (DEPTH, n_pool, PAGE_SIZE, A_HEADS, A_HEAD_DIM), 1.0)
    cache_v = nrm(ks[3], (DEPTH, n_pool, PAGE_SIZE, A_HEADS, A_HEAD_DIM), 1.0)
    state_wkv = nrm(ks[4], (DEPTH, DEC_BATCH, R_HEADS, R_HEAD_DIM, R_HEAD_DIM), 0.3)
    state_shift = nrm(ks[5], (DEPTH, DEC_BATCH, R_PROJ), 1.0)
    page_table = jax.random.permutation(ks[6], n_pool)[:DEC_BATCH * n_pages].reshape(DEC_BATCH, n_pages).astype(jnp.int32)
    return {
        'x_prompt': x_prompt,
        'x_sample': x_sample,
        'cache_k': cache_k,
        'cache_v': cache_v,
        'state_wkv': state_wkv,
        'state_shift': state_shift,
        'page_table': page_table,
        'norm_attn': 1.0 + nrm(ks[7], (DEPTH, D_MODEL), 0.02),
        'w_in': nrm(ks[8], (DEPTH, D_MODEL, PROJ_WIDTH), D_MODEL ** -0.5),
        'w_up_a': nrm(ks[9], (DEPTH, A_WIDTH, D_MODEL), A_WIDTH ** -0.5),
        'mu_shift': uni(ks[10], (DEPTH, R_PROJ), 0.1, 0.9),
        'w0': uni(ks[11], (DEPTH, R_WIDTH), -5.0, 0.5),
        'w_dec2': nrm(ks[12], (DEPTH, LORA_DECAY, R_WIDTH), 0.1 * LORA_DECAY ** -0.5),
        'a0': nrm(ks[13], (DEPTH, R_WIDTH), 0.1),
        'w_a2': nrm(ks[14], (DEPTH, LORA_ICLR, R_WIDTH), LORA_ICLR ** -0.5),
        'w_g2': nrm(ks[15], (DEPTH, LORA_GATE, R_WIDTH), LORA_GATE ** -0.5),
        'k_k': 0.85 + nrm(ks[16], (DEPTH, R_WIDTH), 0.05),
        'k_a': 1.0 + nrm(ks[17], (DEPTH, R_WIDTH), 0.05),
        'r_k': nrm(ks[18], (DEPTH, R_HEADS, R_HEAD_DIM), 0.1),
        'ln_x_w': 1.0 + nrm(ks[19], (DEPTH, R_WIDTH), 0.02),
        'ln_x_b': nrm(ks[20], (DEPTH, R_WIDTH), 0.02),
        'w_up_b': nrm(ks[21], (DEPTH, R_WIDTH, D_MODEL), R_WIDTH ** -0.5),
        'w_out': nrm(ks[22], (DEPTH, D_MODEL, D_MODEL), D_MODEL ** -0.5),
        'norm_ffn': 1.0 + nrm(ks[23], (DEPTH, D_MODEL), 0.02),
        'ffn_w1': nrm(ks[24], (N_DENSE, D_MODEL, D_FF), D_MODEL ** -0.5),
        'ffn_w3': nrm(ks[25], (N_DENSE, D_MODEL, D_FF), D_MODEL ** -0.5),
        'ffn_w2': nrm(ks[26], (N_DENSE, D_FF, D_MODEL), D_FF ** -0.5),
        'router': nrm(ks[27], (N_MOE, D_MODEL, N_EXPERTS), D_MODEL ** -0.5),
        'moe_w1': nrm(ks[28], (N_MOE, N_EXPERTS, D_MODEL, D_FF_EXPERT), D_MODEL ** -0.5),
        'moe_w3': nrm(ks[29], (N_MOE, N_EXPERTS, D_MODEL, D_FF_EXPERT), D_MODEL ** -0.5),
        'moe_w2': nrm(ks[30], (N_MOE, N_EXPERTS, D_FF_EXPERT, D_MODEL), D_FF_EXPERT ** -0.5),
        'norm_final': 1.0 + nrm(ks[31], (D_MODEL,), 0.02),
    }


def reference(x_prompt, x_sample, cache_k, cache_v, state_wkv, state_shift, page_table, norm_attn, w_in, w_up_a,
              mu_shift, w0, w_dec2, a0, w_a2, w_g2, k_k, k_a, r_k, ln_x_w, ln_x_b, w_up_b, w_out, norm_ffn,
              ffn_w1, ffn_w3, ffn_w2, router, moe_w1, moe_w3, moe_w2, norm_final):
    xp, xs = x_prompt, x_sample
    bp, bs = xp.shape[0], xs.shape[0]
    n_pages = page_table.shape[1]
    past_rows = n_pages * cache_k.shape[2]
    pk, pv, pw, psh, sk, sv, sw, ssh = [], [], [], [], [], [], [], []
    for l in range(DEPTH):
        mix_w = (norm_attn[l], w_in[l], w_up_a[l], mu_shift[l], w0[l], w_dec2[l], a0[l], w_a2[l], w_g2[l],
                 k_k[l], k_a[l], r_k[l], ln_x_w[l], ln_x_b[l], w_up_b[l], w_out[l])
        empty = jnp.zeros((bp, 0, A_HEADS, A_HEAD_DIM), xp.dtype)
        s_zero = jnp.zeros((bp, R_HEADS, R_HEAD_DIM, R_HEAD_DIM), jnp.float32)
        p_zero = jnp.zeros((bp, R_PROJ), xp.dtype)
        xp, k_n, v_n, s_n, sh_n = hybrid_mixer(xp, empty, empty, s_zero, p_zero, *mix_w)
        pk.append(k_n); pv.append(v_n); pw.append(s_n); psh.append(sh_n)
        k_past = cache_k[l][page_table].reshape(bs, past_rows, A_HEADS, A_HEAD_DIM)
        v_past = cache_v[l][page_table].reshape(bs, past_rows, A_HEADS, A_HEAD_DIM)
        xs, k_n, v_n, s_n, sh_n = hybrid_mixer(xs, k_past, v_past, state_wkv[l], state_shift[l], *mix_w)
        sk.append(k_n); sv.append(v_n); sw.append(s_n); ssh.append(sh_n)
        xp = channel_mixer(xp, l, norm_ffn, ffn_w1, ffn_w3, ffn_w2, router, moe_w1, moe_w3, moe_w2)
        xs = channel_mixer(xs, l, norm_ffn, ffn_w1, ffn_w3, ffn_w2, router, moe_w1, moe_w3, moe_w2)
    y_prompt = rms_norm(xp, norm_final)
    y_sample = rms_norm(xs, norm_final)
    return (y_prompt, y_sample, jnp.stack(pk), jnp.stack(pv), jnp.stack(pw), jnp.stack(psh),
            jnp.stack(sk), jnp.stack(sv), jnp.stack(sw), jnp.stack(ssh))
```

```python
import functools

import jax
import jax.numpy as jnp
from jax import lax
from jax.experimental import pallas as pl
from jax.experimental.pallas import tpu as pltpu

F32 = jnp.float32
BF16 = jnp.bfloat16

A_HEADS = 8
HEAD_DIM = 64
A_WIDTH = 512
R_WIDTH = 512
MOBA_BLOCK = 256
MOBA_TOPK = 3
LORA_DECAY = 64
LORA_ICLR = 64
LORA_GATE = 160
LORA_PAD = 384
RKV_W = 3 * R_WIDTH
GN_EPS = 64e-5
RMS_EPS = 1e-6
MOE_TOPK = 2
RWKV_CHUNK = 64
NEG = -1e30
VMEM_LIMIT = 56 * 1024 * 1024


def _dot(a, b):
    return jnp.dot(a, b, preferred_element_type=F32)


def _dot_nt(a, b):
    return lax.dot_general(a, b, (((1,), (1,)), ((), ())), preferred_element_type=F32)


def _dot_tn(a, b):
    return lax.dot_general(a, b, (((0,), (0,)), ((), ())), preferred_element_type=F32)


def _split2(x):
    hi = x.astype(BF16)
    lo = (x - hi.astype(F32)).astype(BF16)
    return hi, lo


def _split3(x):
    hi = x.astype(BF16)
    r1 = x - hi.astype(F32)
    mid = r1.astype(BF16)
    lo = (r1 - mid.astype(F32)).astype(BF16)
    return hi, mid, lo


def _dot_x2(x, m):
    hi, lo = _split2(x)
    return _dot(hi, m) + _dot(lo, m)


def _dot_x3(x, m):
    hi, mid, lo = _split3(x)
    return _dot(hi, m) + _dot(mid, m) + _dot(lo, m)


def _mm(a, b, passes):
    if passes == 1:
        return _dot(a.astype(BF16), b.astype(BF16))
    ah, al = _split2(a)
    bh, bl = _split2(b)
    return _dot(ah, bh) + _dot(al, bh) + _dot(ah, bl)


def _rms(x, gain):
    return x * lax.rsqrt(jnp.mean(x * x, axis=-1, keepdims=True) + RMS_EPS) * gain


def _cparams(sem):
    return pltpu.CompilerParams(dimension_semantics=sem, vmem_limit_bytes=VMEM_LIMIT)


def _proj_kernel(x_ref, gain_ref, wqkv_ref, wrkv_ref, wlora_ref,
                 q_ref, kb_ref, vb_ref, k_ref, v_ref, rkv_ref, lora_ref):
    h = _rms(x_ref[...], gain_ref[...]).astype(BF16)
    q = _dot(h, wqkv_ref[:, 0:A_WIDTH])
    q_ref[...] = (q * (HEAD_DIM ** -0.5)).astype(BF16)
    k = _dot(h, wqkv_ref[:, A_WIDTH:2 * A_WIDTH])
    k_ref[...] = k
    kb_ref[...] = k.astype(BF16)
    v = _dot(h, wqkv_ref[:, 2 * A_WIDTH:3 * A_WIDTH])
    v_ref[...] = v
    vb_ref[...] = v.astype(BF16)
    rkv_ref[...] = _dot(h, wrkv_ref[...])
    lora_ref[...] = _dot(h, wlora_ref[...])


def _proj(x, gain, wqkv, wrkv, wlora):
    n, d = x.shape
    tm = min(512, n)
    row = lambda w: pl.BlockSpec((tm, w), lambda i: (i, 0))
    full = lambda a: pl.BlockSpec(a.shape, lambda i: (0, 0))
    return pl.pallas_call(
        _proj_kernel,
        grid=(n // tm,),
        in_specs=[row(d), full(gain), full(wqkv), full(wrkv), full(wlora)],
        out_specs=[row(A_WIDTH)] * 5 + [row(RKV_W), row(LORA_PAD)],
        out_shape=[jax.ShapeDtypeStruct((n, A_WIDTH), BF16)] * 3
        + [jax.ShapeDtypeStruct((n, A_WIDTH), F32)] * 2
        + [jax.ShapeDtypeStruct((n, RKV_W), F32), jax.ShapeDtypeStruct((n, LORA_PAD), F32)],
        compiler_params=_cparams(("parallel",)),
        name="proj",
    )(x, gain, wqkv, wrkv, wlora)


def _kmean_kernel(k_ref, o_ref):
    g = o_ref.shape[0]
    x = k_ref[...].reshape(g, MOBA_BLOCK, A_WIDTH)
    o_ref[...] = jnp.sum(x, axis=1) * (1.0 / MOBA_BLOCK)


def _kmean(k):
    n = k.shape[0]
    nblk = n // MOBA_BLOCK
    g = min(8, nblk)
    return pl.pallas_call(
        _kmean_kernel,
        grid=(nblk // g,),
        in_specs=[pl.BlockSpec((g * MOBA_BLOCK, A_WIDTH), lambda i: (i, 0))],
        out_specs=pl.BlockSpec((g, A_WIDTH), lambda i: (i, 0)),
        out_shape=jax.ShapeDtypeStruct((nblk, A_WIDTH), F32),
        compiler_params=_cparams(("parallel",)),
        name="kmean",
    )(k)


def _moba_kernel(qt_ref, k_ref, vt_ref, km_ref, o_ref, bias_ref, *, nb):
    i = pl.program_id(2)
    blk = MOBA_BLOCK
    qt = qt_ref[0]
    row = lax.broadcasted_iota(jnp.int32, qt.shape, 0)
    km_hi, km_lo = _split2(km_ref[0])
    bid = lax.broadcasted_iota(jnp.int32, (nb, blk), 0)
    qms = []
    for hh in range(2):
        qm = jnp.where((row >= hh * HEAD_DIM) & (row < (hh + 1) * HEAD_DIM), qt, jnp.zeros_like(qt))
        qms.append(qm)
        gate = _dot(km_hi, qm) + _dot(km_lo, qm)
        gate = jnp.where(bid < i, gate, -jnp.inf)
        rank = jnp.zeros((nb, blk), F32)
        for m in range(nb):
            gm = gate[m:m + 1, :]
            beats = (gm > gate) | ((gm == gate) & (bid > m))
            rank = rank + jnp.where(beats, 1.0, 0.0)
        sel = (bid < i) & (rank < float(MOBA_TOPK))
        bias_ref[hh] = jnp.where(sel, 0.0, NEG)

    kd = k_ref[0, i]
    vd = vt_ref[0, i]
    kio = lax.broadcasted_iota(jnp.int32, (blk, blk), 0)
    qio = lax.broadcasted_iota(jnp.int32, (blk, blk), 1)
    causal = kio <= qio
    carry = []
    for hh in range(2):
        s = jnp.where(causal, _dot(kd, qms[hh]), NEG)
        m = jnp.max(s, axis=0, keepdims=True)
        p = jnp.exp(s - m)
        l = jnp.sum(p, axis=0, keepdims=True)
        acc = _dot(vd[hh * HEAD_DIM:(hh + 1) * HEAD_DIM, :], p.astype(BF16))
        carry += [m, l, acc]

    def body(j, carry):
        kj = k_ref[0, j]
        vj = vt_ref[0, j]
        out = []
        for hh in range(2):
            m, l, acc = carry[3 * hh:3 * hh + 3]
            s = _dot(kj, qms[hh]) + bias_ref[hh, pl.ds(j, 1), :]
            m_new = jnp.maximum(m, jnp.max(s, axis=0, keepdims=True))
            alpha = jnp.exp(m - m_new)
            p = jnp.exp(s - m_new)
            l = alpha * l + jnp.sum(p, axis=0, keepdims=True)
            acc = alpha * acc + _dot(vj[hh * HEAD_DIM:(hh + 1) * HEAD_DIM, :], p.astype(BF16))
            out += [m_new, l, acc]
        return tuple(out)

    carry = lax.fori_loop(0, i, body, tuple(carry))
    o = jnp.concatenate([carry[2] / carry[1], carry[5] / carry[4]], axis=0)
    o_ref[0] = o.T.astype(BF16)


def _moba_prompt(qb, kb, vb, kmean, bsz, seq):
    nb = seq // MOBA_BLOCK
    qt = qb.reshape(bsz, seq, A_WIDTH).transpose(0, 2, 1)
    k4 = kb.reshape(bsz, nb, MOBA_BLOCK, A_WIDTH)
    vt4 = vb.reshape(bsz, nb, MOBA_BLOCK, A_WIDTH).transpose(0, 1, 3, 2)
    km = kmean.reshape(bsz, nb, A_WIDTH)
    out = pl.pallas_call(
        functools.partial(_moba_kernel, nb=nb),
        grid=(bsz, A_HEADS // 2, nb),
        in_specs=[
            pl.BlockSpec((1, 128, MOBA_BLOCK), lambda b, h, i: (b, h, i)),
            pl.BlockSpec((1, nb, MOBA_BLOCK, 128), lambda b, h, i: (b, 0, 0, h)),
            pl.BlockSpec((1, nb, 128, MOBA_BLOCK), lambda b, h, i: (b, 0, h, 0)),
            pl.BlockSpec((1, nb, 128), lambda b, h, i: (b, 0, h)),
        ],
        out_specs=pl.BlockSpec((1, MOBA_BLOCK, 128), lambda b, h, i: (b, i, h)),
        out_shape=jax.ShapeDtypeStruct((bsz, seq, A_WIDTH), BF16),
        scratch_shapes=[pltpu.VMEM((2, nb, MOBA_BLOCK), F32)],
        compiler_params=_cparams(("parallel", "parallel", "arbitrary")),
        name="moba_prompt",
    )(qt, k4, vt4, km)
    return out.reshape(bsz * seq, A_WIDTH)


def _kmean_paged_kernel(pt_ref, ck_ref, o_ref, buf, sem, *, layer, n_pages):
    b = pl.program_id(0)
    for p in range(n_pages):
        pltpu.make_async_copy(ck_ref.at[layer, pt_ref[b, p]], buf.at[p], sem.at[p]).start()

    def body(n, _):
        tot = jnp.zeros((1, A_WIDTH), F32)
        for e in range(2):
            pltpu.make_async_copy(ck_ref.at[layer, 0], buf.at[2 * n + e], sem.at[2 * n + e]).wait()
            tot = tot + jnp.sum(buf[2 * n + e], axis=0, keepdims=True)
        o_ref[0, pl.ds(n, 1), :] = tot * (1.0 / MOBA_BLOCK)
        return 0

    lax.fori_loop(0, n_pages // 2, body, 0)


def _kmean_paged(page_table, cache_k4, layer):
    bs, n_pages = page_table.shape
    page = cache_k4.shape[2]
    nb = n_pages // 2
    return pl.pallas_call(
        functools.partial(_kmean_paged_kernel, layer=layer, n_pages=n_pages),
        grid_spec=pltpu.PrefetchScalarGridSpec(
            num_scalar_prefetch=1,
            grid=(bs,),
            in_specs=[pl.BlockSpec(memory_space=pl.ANY)],
            out_specs=pl.BlockSpec((1, nb, A_WIDTH), lambda b, pt: (b, 0, 0)),
            scratch_shapes=[pltpu.VMEM((n_pages, page, A_WIDTH), F32), pltpu.SemaphoreType.DMA((n_pages,))],
        ),
        out_shape=jax.ShapeDtypeStruct((bs, nb, A_WIDTH), F32),
        compiler_params=_cparams(("arbitrary",)),
        name="kmean_paged",
    )(page_table, cache_k4)


def _select_kernel(q_ref, km_ref, gh_ref, o_ref, *, bs, nb):
    bid = lax.broadcasted_iota(jnp.int32, (nb, 128), 0)
    bidf = bid.astype(F32)

    def body(b, _):
        prod = km_ref[b] * q_ref[pl.ds(b, 1), :]
        gate = _dot_x3(prod, gh_ref[...])
        rank = jnp.zeros((nb, 128), F32)
        for m in range(nb):
            gm = gate[m:m + 1, :]
            beats = (gm > gate) | ((gm == gate) & (bid > m))
            rank = rank + jnp.where(beats, 1.0, 0.0)
        rows = [jnp.sum(jnp.where(rank == float(r), bidf, 0.0), axis=0, keepdims=True) for r in range(MOBA_TOPK)]
        rows.append(jnp.zeros((8 - MOBA_TOPK, 128), F32))
        o_ref[b] = jnp.concatenate(rows, axis=0).astype(jnp.int32)
        return 0

    lax.fori_loop(0, bs, body, 0)


def _select(qf, kmean_s):
    bs, nb, _ = kmean_s.shape
    head_of = jnp.arange(A_WIDTH, dtype=jnp.int32) // HEAD_DIM
    gh = (head_of[:, None] == jnp.arange(128, dtype=jnp.int32)[None, :]).astype(BF16)
    return pl.pallas_call(
        functools.partial(_select_kernel, bs=bs, nb=nb),
        out_shape=jax.ShapeDtypeStruct((bs, 8, 128), jnp.int32),
        compiler_params=pltpu.CompilerParams(vmem_limit_bytes=VMEM_LIMIT),
        name="moba_select",
    )(qf, kmean_s, gh)


def _decode_kernel(pt_ref, sel_ref, q_ref, kn_ref, vn_ref, ck_ref, cv_ref, o_ref, kbuf, vbuf, sem, *, layer):
    b = pl.program_id(0)
    copies = []
    for h in range(A_HEADS):
        for r in range(MOBA_TOPK):
            blk = sel_ref[b, h * MOBA_TOPK + r]
            for e in range(2):
                page = pt_ref[b, 2 * blk + e]
                for src, dst, s in ((ck_ref, kbuf, 0), (cv_ref, vbuf, 1)):
                    c = pltpu.make_async_copy(src.at[layer, page, :, pl.ds((h // 2) * 128, 128)],
                                              dst.at[h, 2 * r + e], sem.at[s])
                    c.start()
                    copies.append(c)
    for c in copies:
        c.wait()
    q = q_ref[0]
    kn = kn_ref[0]
    vn = vn_ref[0]
    lane = lax.broadcasted_iota(jnp.int32, (1, 128), 1)
    outs = []
    for pair in range(A_HEADS // 2):
        ls = slice(pair * 128, (pair + 1) * 128)
        o_pair = jnp.zeros((1, 128), F32)
        for hh in range(2):
            h = 2 * pair + hh
            own = (lane >= hh * HEAD_DIM) & (lane < (hh + 1) * HEAD_DIM)
            qh = jnp.where(own, q[:, ls], 0.0)
            q8 = jnp.broadcast_to(qh, (8, 128)).astype(BF16)
            kh = kbuf[h].reshape(-1, 128).astype(BF16)
            vh = vbuf[h].reshape(-1, 128).astype(BF16)
            s = _dot_nt(q8, kh)[0:1]
            s_self = jnp.sum(qh * kn[:, ls], axis=-1, keepdims=True)
            m = jnp.maximum(jnp.max(s, axis=-1, keepdims=True), s_self)
            p = jnp.exp(s - m)
            p_self = jnp.exp(s_self - m)
            l = jnp.sum(p, axis=-1, keepdims=True) + p_self
            pv = _dot(jnp.broadcast_to(p, (8, p.shape[1])).astype(BF16), vh)[0:1]
            o_pair = o_pair + jnp.where(own, (pv + p_self * vn[:, ls]) / l, 0.0)
        outs.append(o_pair)
    o_ref[0] = jnp.concatenate(outs, axis=-1)


def _decode_attn(page_table, sel, qf, k_new, v_new, cache_k4, cache_v4, layer):
    bs = qf.shape[0]
    page = cache_k4.shape[2]
    row = pl.BlockSpec((1, 1, A_WIDTH), lambda b, pt, sl: (b, 0, 0))
    out = pl.pallas_call(
        functools.partial(_decode_kernel, layer=layer),
        grid_spec=pltpu.PrefetchScalarGridSpec(
            num_scalar_prefetch=2,
            grid=(bs,),
            in_specs=[row, row, row, pl.BlockSpec(memory_space=pl.ANY), pl.BlockSpec(memory_space=pl.ANY)],
            out_specs=row,
            scratch_shapes=[pltpu.VMEM((A_HEADS, 2 * MOBA_TOPK, page, 128), F32),
                            pltpu.VMEM((A_HEADS, 2 * MOBA_TOPK, page, 128), F32),
                            pltpu.SemaphoreType.DMA((2,))],
        ),
        out_shape=jax.ShapeDtypeStruct((bs, 1, A_WIDTH), F32),
        compiler_params=_cparams(("arbitrary",)),
        name="moba_decode",
    )(page_table, sel, qf.reshape(bs, 1, A_WIDTH), k_new.reshape(bs, 1, A_WIDTH), v_new.reshape(bs, 1, A_WIDTH),
      cache_k4, cache_v4)
    return out.reshape(bs, A_WIDTH)


def _rwkv_kernel(rkv_ref, lora_ref, prkv_ref, plora_ref, s0_ref, mu_rkv_ref, mu_lora_ref, w0_ref, wdec_ref,
                 a0_ref, wa_ref, wg_ref, kk_ref, ka_ref, rk_ref, lnw_ref, lnb_ref, g_ref, tri_ref,
                 ob_ref, s_ref, c_rkv, c_lora, s_an, s_r, s_b, s_k, s_v, s_ld, s_y, *, T, C, n_valid, passes):
    i = pl.program_id(1)

    @pl.when(i == 0)
    def _():
        c_rkv[...] = prkv_ref[0]
        c_lora[...] = plora_ref[0]
        s_ref[0] = s0_ref[0]

    x = rkv_ref[0]
    xl = lora_ref[0]
    rowi = lax.broadcasted_iota(jnp.int32, (T, 1), 0)

    def shifted(cur, prev):
        return jnp.where(rowi == 0, prev, pltpu.roll(cur, 1, 0))

    xs = shifted(x, c_rkv[...])
    xls = shifted(xl, c_lora[...])
    c_rkv[...] = x[T - 1:T]
    c_lora[...] = xl[T - 1:T]
    pm = x + (xs - x) * mu_rkv_ref[...]
    pml = xl + (xls - xl) * mu_lora_ref[...]
    r = pm[:, 0:R_WIDTH]
    k = pm[:, R_WIDTH:2 * R_WIDTH]
    v = pm[:, 2 * R_WIDTH:3 * R_WIDTH]
    t01 = pml[:, 0:128]
    dec_arg = w0_ref[...] + _dot(jnp.tanh(t01).astype(BF16), wdec_ref[...])
    sp = jnp.maximum(-dec_arg, 0.0) + jnp.log(1.0 + jnp.exp(-jnp.abs(dec_arg)))
    ld = -jnp.exp(-sp - 0.5)
    a = jax.nn.sigmoid(a0_ref[...] + _dot(t01.astype(BF16), wa_ref[...]))
    g = _dot(jax.nn.sigmoid(pml[:, 128:LORA_PAD]).astype(BF16), wg_ref[...])
    gmat = g_ref[...]
    kk0 = k * kk_ref[...]
    kk = kk0 * lax.rsqrt(jnp.maximum(_dot_x2(kk0 * kk0, gmat), 1e-24))
    kmod = k * (1.0 + (a - 1.0) * ka_ref[...])
    bb = kk * a
    if n_valid < T:
        valid = rowi < n_valid
        ld = jnp.where(valid, ld, 0.0)
        kk = jnp.where(valid, kk, 0.0)
        bb = jnp.where(valid, bb, 0.0)
        kmod = jnp.where(valid, kmod, 0.0)
    s_an[...] = -kk
    s_r[...] = r
    s_b[...] = bb
    s_k[...] = kmod
    s_v[...] = v
    s_ld[...] = ld

    tri = tri_ref[...]
    lane = lax.broadcasted_iota(jnp.int32, (C, 128), 1)
    in_h0 = lane < HEAD_DIM
    ri = lax.broadcasted_iota(jnp.int32, (2 * C, 2 * C), 0)
    ci = lax.broadcasted_iota(jnp.int32, (2 * C, 2 * C), 1)
    same = (ri >= C) == (ci >= C)
    strict = same & (ri > ci)
    incl = same & (ri >= ci)
    eye = jnp.where(ri == ci, 1.0, 0.0)

    def stack_masked(t):
        return jnp.concatenate([jnp.where(in_h0, t, 0.0), jnp.where(in_h0, 0.0, t)], axis=0)

    def chunk(c, _):
        rs = pl.ds(pl.multiple_of(c * C, C), C)
        ld_c = s_ld[rs, :]
        ld_hi, ld_mid, ld_lo = _split3(ld_c)
        cum = _dot(tri, ld_hi) + _dot(tri, ld_mid) + _dot(tri, ld_lo)
        cum_last = cum[C - 1:C]
        e_in = jnp.exp(cum)
        e_ex = jnp.exp(cum - ld_c)
        e_neg = jnp.exp(-cum)
        e_end = jnp.exp(cum_last - cum)
        g_end = jnp.exp(cum_last)
        at = s_an[rs, :] * e_ex
        rt = s_r[rs, :] * e_in
        bt = s_b[rs, :] * e_neg
        kt = s_k[rs, :] * e_neg
        bg = s_b[rs, :] * e_end
        kg = s_k[rs, :] * e_end
        vv = s_v[rs, :]
        for p in range(4):
            ls = slice(p * 128, (p + 1) * 128)
            at_s = stack_masked(at[:, ls])
            rt_s = stack_masked(rt[:, ls])
            vv_s = stack_masked(vv[:, ls])
            bg_s = stack_masked(bg[:, ls])
            kg_s = stack_masked(kg[:, ls])
            bt2 = jnp.concatenate([bt[:, ls], bt[:, ls]], axis=0).astype(BF16)
            kt2 = jnp.concatenate([kt[:, ls], kt[:, ls]], axis=0).astype(BF16)
            at_b = at_s.astype(BF16)
            rt_b = rt_s.astype(BF16)
            a_ab = jnp.where(strict, _dot_nt(at_b, bt2), 0.0)
            a_ak = jnp.where(strict, _dot_nt(at_b, kt2), 0.0)
            a_rb = jnp.where(incl, _dot_nt(rt_b, bt2), 0.0)
            a_rk = jnp.where(incl, _dot_nt(rt_b, kt2), 0.0)
            inv = eye + a_ab
            pw = a_ab
            for _ in range(5):
                pw = _mm(pw, pw, passes)
                inv = inv + _mm(pw, inv, passes)
            st = s_ref[0, p]
            st_b = st.astype(BF16)
            rhs = _dot_nt(at_b, st_b) + _mm(a_ak, vv_s, passes)
            u = _mm(inv, rhs, passes)
            y_s = _dot_nt(rt_b, st_b) + _mm(a_rb, u, passes) + _mm(a_rk, vv_s, passes)
            s_y[rs, ls] = y_s[0:C] + y_s[C:2 * C]
            s_ref[0, p] = st * g_end[:, ls] + _dot_tn(u.astype(BF16), bg_s.astype(BF16)) + _dot_tn(vv_s.astype(BF16), kg_s.astype(BF16))
        return 0

    lax.fori_loop(0, T // C, chunk, 0)

    y = s_y[...]
    mean = _dot_x2(y, gmat) * (1.0 / HEAD_DIM)
    d = y - mean
    var = _dot_x2(d * d, gmat) * (1.0 / HEAD_DIM)
    yn = d * lax.rsqrt(var + GN_EPS) * lnw_ref[...] + lnb_ref[...]
    bonus = _dot_x2(r * kmod * rk_ref[...], gmat) * v
    ob_ref[0] = ((yn + bonus) * g).astype(BF16)


def _rwkv(rkv, lora, prev_rkv, prev_lora, s0_bd, wts, *, n_valid, passes=1):
    bsz, seq, _ = rkv.shape
    C = RWKV_CHUNK
    T = min(256, seq)
    head_of = jnp.arange(R_WIDTH, dtype=jnp.int32) // HEAD_DIM
    gmat = (head_of[:, None] == head_of[None, :]).astype(BF16)
    tri = (jnp.arange(C)[:, None] >= jnp.arange(C)[None, :]).astype(BF16)
    consts = list(wts) + [gmat, tri]
    seq_spec = lambda w: pl.BlockSpec((1, T, w), lambda b, i: (b, i, 0))
    per_b = lambda a: pl.BlockSpec((1,) + a.shape[1:], lambda b, i: (b,) + (0,) * (a.ndim - 1))
    full = lambda a: pl.BlockSpec(a.shape, lambda b, i: (0,) * a.ndim)
    big = lambda: pltpu.VMEM((T, R_WIDTH), F32)
    return pl.pallas_call(
        functools.partial(_rwkv_kernel, T=T, C=C, n_valid=n_valid, passes=passes),
        grid=(bsz, seq // T),
        in_specs=[seq_spec(RKV_W), seq_spec(LORA_PAD), per_b(prev_rkv), per_b(prev_lora), per_b(s0_bd)]
        + [full(a) for a in consts],
        out_specs=[seq_spec(R_WIDTH), per_b(s0_bd)],
        out_shape=[jax.ShapeDtypeStruct((bsz, seq, R_WIDTH), BF16), jax.ShapeDtypeStruct(s0_bd.shape, F32)],
        scratch_shapes=[pltpu.VMEM((1, RKV_W), F32), pltpu.VMEM((1, LORA_PAD), F32)] + [big() for _ in range(7)],
        compiler_params=_cparams(("parallel", "arbitrary")),
        name="rwkv",
    )(rkv, lora, prev_rkv, prev_lora, s0_bd, *consts)


def _state_to_bd(s):
    b = s.shape[0]
    s = s.reshape(b, 4, 2, HEAD_DIM, HEAD_DIM)
    z = jnp.zeros_like(s[:, :, 0])
    top = jnp.concatenate([s[:, :, 0], z], axis=-1)
    bot = jnp.concatenate([z, s[:, :, 1]], axis=-1)
    return jnp.concatenate([top, bot], axis=-2)


def _state_from_bd(s):
    b = s.shape[0]
    h0 = s[:, :, :HEAD_DIM, :HEAD_DIM]
    h1 = s[:, :, HEAD_DIM:, HEAD_DIM:]
    return jnp.stack([h0, h1], axis=2).reshape(b, 8, HEAD_DIM, HEAD_DIM)


def _merge_kernel(x_ref, gain_ref, wg_ref, oa_ref, ob_ref, wa_ref, wb_ref, wo_ref, o_ref):
    x = x_ref[...]
    d = x.shape[1]
    h = _rms(x, gain_ref[...]).astype(BF16)
    gates = jax.nn.sigmoid(_dot(h, wg_ref[...]))
    ya = _dot(oa_ref[...], wa_ref[...])
    yb = _dot(ob_ref[...], wb_ref[...])
    merged = gates[:, :d] * ya + gates[:, d:] * yb
    o_ref[...] = x + _dot(merged.astype(BF16), wo_ref[...])


def _merge(x, gain, wg, oa, ob, wa, wb, wo):
    n, d = x.shape
    tm = min(512, n)
    row = lambda w: pl.BlockSpec((tm, w), lambda i: (i, 0))
    full = lambda a: pl.BlockSpec(a.shape, lambda i: (0, 0))
    return pl.pallas_call(
        _merge_kernel,
        grid=(n // tm,),
        in_specs=[row(d), full(gain), full(wg), row(A_WIDTH), row(R_WIDTH), full(wa), full(wb), full(wo)],
        out_specs=row(d),
        out_shape=jax.ShapeDtypeStruct((n, d), F32),
        compiler_params=_cparams(("parallel",)),
        name="merge",
    )(x, gain, wg, oa, ob, wa, wb, wo)


def _ffn_kernel(x_ref, gain_ref, w1_ref, w3_ref, w2_ref, o_ref, h_scr, acc):
    f = pl.program_id(1)

    @pl.when(f == 0)
    def _():
        h_scr[...] = _rms(x_ref[...], gain_ref[...]).astype(BF16)
        acc[...] = x_ref[...]

    h = h_scr[...]
    t = jax.nn.silu(_dot(h, w1_ref[...])) * _dot(h, w3_ref[...])
    acc[...] += _dot(t.astype(BF16), w2_ref[...])

    @pl.when(f == pl.num_programs(1) - 1)
    def _():
        o_ref[...] = acc[...]


def _ffn(x, gain, w1, w3, w2):
    n, d = x.shape
    dff = w1.shape[1]
    tm = min(512, n)
    tf = dff // 2
    return pl.pallas_call(
        _ffn_kernel,
        grid=(n // tm, dff // tf),
        in_specs=[pl.BlockSpec((tm, d), lambda i, f: (i, 0)), pl.BlockSpec((1, d), lambda i, f: (0, 0)),
                  pl.BlockSpec((d, tf), lambda i, f: (0, f)), pl.BlockSpec((d, tf), lambda i, f: (0, f)),
                  pl.BlockSpec((tf, d), lambda i, f: (f, 0))],
        out_specs=pl.BlockSpec((tm, d), lambda i, f: (i, 0)),
        out_shape=jax.ShapeDtypeStruct((n, d), F32),
        scratch_shapes=[pltpu.VMEM((tm, d), BF16), pltpu.VMEM((tm, d), F32)],
        compiler_params=_cparams(("parallel", "arbitrary")),
        name="ffn",
    )(x, gain, w1, w3, w2)


def _router_kernel(x_ref, gain_ref, wr_ref, h_ref, o_ref, *, n_exp):
    h = _rms(x_ref[...], gain_ref[...])
    h_ref[...] = h.astype(BF16)
    hh, hl = _split2(h)
    wh = wr_ref[0]
    wl = wr_ref[1]
    logits = _dot(hh, wh) + _dot(hl, wh) + _dot(hh, wl)
    lane = lax.broadcasted_iota(jnp.int32, logits.shape, 1)
    lanef = lane.astype(F32)
    logits = jnp.where(lane < n_exp, logits, -jnp.inf)
    m1 = jnp.max(logits, axis=-1, keepdims=True)
    e1 = jnp.min(jnp.where(logits == m1, lanef, 1e9), axis=-1, keepdims=True)
    rest = jnp.where(lanef == e1, -jnp.inf, logits)
    m2 = jnp.max(rest, axis=-1, keepdims=True)
    e2 = jnp.min(jnp.where(rest == m2, lanef, 1e9), axis=-1, keepdims=True)
    z = jnp.exp(m2 - m1)
    g1 = 1.0 / (1.0 + z)
    g2 = z / (1.0 + z)
    res = jnp.where(lane == 0, e1, jnp.where(lane == 1, e2, jnp.where(lane == 2, g1, jnp.where(lane == 3, g2, 0.0))))
    o_ref[...] = res[:, 0:8]


def _router(x, gain, w_router):
    n, d = x.shape
    n_exp = w_router.shape[1]
    tm = min(512, n)
    wp = jnp.zeros((d, 128), F32).at[:, :n_exp].set(w_router)
    wh = wp.astype(BF16)
    wl = (wp - wh.astype(F32)).astype(BF16)
    w2 = jnp.stack([wh, wl])
    return pl.pallas_call(
        functools.partial(_router_kernel, n_exp=n_exp),
        grid=(n // tm,),
        in_specs=[pl.BlockSpec((tm, d), lambda i: (i, 0)), pl.BlockSpec((1, d), lambda i: (0, 0)),
                  pl.BlockSpec((2, d, 128), lambda i: (0, 0, 0))],
        out_specs=[pl.BlockSpec((tm, d), lambda i: (i, 0)), pl.BlockSpec((tm, 8), lambda i: (i, 0))],
        out_shape=[jax.ShapeDtypeStruct((n, d), BF16), jax.ShapeDtypeStruct((n, 8), F32)],
        compiler_params=_cparams(("parallel",)),
        name="router",
    )(x, gain, w2)


def _experts_kernel(be_ref, xb_ref, rg_ref, w1_ref, w3_ref, w2_ref, o_ref):
    x = xb_ref[...]
    t = jax.nn.silu(_dot(x, w1_ref[0])) * _dot(x, w3_ref[0])
    o_ref[...] = _dot(t.astype(BF16), w2_ref[0]) * rg_ref[...]


def _experts(blk_e, xb, row_gate, w1, w3, w2, rb):
    r, d = xb.shape
    dffe = w1.shape[2]
    return pl.pallas_call(
        _experts_kernel,
        grid_spec=pltpu.PrefetchScalarGridSpec(
            num_scalar_prefetch=1,
            grid=(r // rb,),
            in_specs=[pl.BlockSpec((rb, d), lambda i, be: (i, 0)), pl.BlockSpec((rb, 1), lambda i, be: (i, 0)),
                      pl.BlockSpec((1, d, dffe), lambda i, be: (be[i], 0, 0)),
                      pl.BlockSpec((1, d, dffe), lambda i, be: (be[i], 0, 0)),
                      pl.BlockSpec((1, dffe, d), lambda i, be: (be[i], 0, 0))],
            out_specs=pl.BlockSpec((rb, d), lambda i, be: (i, 0)),
        ),
        out_shape=jax.ShapeDtypeStruct((r, d), F32),
        compiler_params=_cparams(("arbitrary",)),
        name="experts",
    )(blk_e, xb, row_gate, w1, w3, w2)


def _moe(x, gain, w_router, w1, w3, w2, rb):
    n, d = x.shape
    n_exp = w_router.shape[1]
    h, route = _router(x, gain, w_router)
    top_e = route[:, 0:MOE_TOPK].astype(jnp.int32)
    gate = route[:, MOE_TOPK:2 * MOE_TOPK]
    na = n * MOE_TOPK
    e_flat = top_e.reshape(na)
    g_flat = gate.reshape(na)
    onehot = (e_flat[:, None] == jnp.arange(n_exp, dtype=jnp.int32)[None, :]).astype(jnp.int32)
    within = jnp.cumsum(onehot, axis=0) - onehot
    counts = jnp.sum(onehot, axis=0)
    padded = (counts + rb - 1) // rb * rb
    pend = jnp.cumsum(padded)
    pstart = pend - padded
    dest = pstart[e_flat] + jnp.sum(within * onehot, axis=1)
    r = (na + n_exp * (rb - 1) + rb - 1) // rb * rb
    tok = jnp.arange(na, dtype=jnp.int32) // MOE_TOPK
    row_tok = jnp.zeros((r,), jnp.int32).at[dest].set(tok)
    row_gate = jnp.zeros((r,), F32).at[dest].set(g_flat)
    blk_start = jnp.arange(r // rb, dtype=jnp.int32) * rb
    blk_e = jnp.minimum(jnp.searchsorted(pend, blk_start, side="right"), n_exp - 1).astype(jnp.int32)
    xb = h[row_tok]
    yb = _experts(blk_e, xb, row_gate[:, None], w1, w3, w2, rb)
    pos = dest.reshape(n, MOE_TOPK)
    return x + yb[pos[:, 0]] + yb[pos[:, 1]]


def _norm_kernel(x_ref, gain_ref, o_ref):
    o_ref[...] = _rms(x_ref[...], gain_ref[...])


def _final_norm(x, gain):
    n, d = x.shape
    tm = min(1024, n)
    return pl.pallas_call(
        _norm_kernel,
        grid=(n // tm,),
        in_specs=[pl.BlockSpec((tm, d), lambda i: (i, 0)), pl.BlockSpec((1, d), lambda i: (0, 0))],
        out_specs=pl.BlockSpec((tm, d), lambda i: (i, 0)),
        out_shape=jax.ShapeDtypeStruct((n, d), F32),
        compiler_params=_cparams(("parallel",)),
        name="final_norm",
    )(x, gain)


def _layer_weights(l, norm_attn, w_in, w_up_a, mu_shift, w0, w_dec2, a0, w_a2, w_g2, k_k, k_a, r_k, ln_x_w, ln_x_b,
                   w_up_b, w_out):
    d = w_in.shape[1]
    wi = w_in[l]
    c0 = 3 * A_WIDTH
    c1 = c0 + RKV_W
    n_lora = LORA_DECAY + LORA_ICLR + LORA_GATE
    c2 = c1 + n_lora
    wqkv = wi[:, :c0].astype(BF16)
    wrkv = wi[:, c0:c1].astype(BF16)
    wlora = jnp.zeros((d, LORA_PAD), BF16).at[:, :n_lora].set(wi[:, c1:c2].astype(BF16))
    wgate = wi[:, c2:].astype(BF16)
    mu = mu_shift[l]
    mu_rkv = mu[:RKV_W][None]
    mu_lora = jnp.zeros((1, LORA_PAD), F32).at[0, :n_lora].set(mu[RKV_W:])
    wdec = jnp.zeros((128, R_WIDTH), BF16).at[:LORA_DECAY].set(w_dec2[l].astype(BF16))
    wa = jnp.zeros((128, R_WIDTH), BF16).at[LORA_DECAY:LORA_DECAY + LORA_ICLR].set(w_a2[l].astype(BF16))
    wg = jnp.zeros((LORA_PAD - 128, R_WIDTH), BF16).at[:LORA_GATE].set(w_g2[l].astype(BF16))
    rw = (mu_rkv, mu_lora, w0[l][None], wdec, a0[l][None], wa, wg, k_k[l][None], k_a[l][None],
          r_k[l].reshape(1, R_WIDTH), ln_x_w[l][None], ln_x_b[l][None])
    return dict(gain=norm_attn[l][None], wqkv=wqkv, wrkv=wrkv, wlora=wlora, wgate=wgate, rw=rw,
                wa=w_up_a[l].astype(BF16), wb=w_up_b[l].astype(BF16), wo=w_out[l].astype(BF16), n_lora=n_lora)


def _mixer_prompt(x, bsz, seq, w):
    qb, kb, vb, k, v, rkv, lora = _proj(x, w["gain"], w["wqkv"], w["wrkv"], w["wlora"])
    oa = _moba_prompt(qb, kb, vb, _kmean(k), bsz, seq)
    zeros = lambda width: jnp.zeros((bsz, 1, width), F32)
    ob, s_bd = _rwkv(rkv.reshape(bsz, seq, RKV_W), lora.reshape(bsz, seq, LORA_PAD), zeros(RKV_W), zeros(LORA_PAD),
                     jnp.zeros((bsz, 4, 128, 128), F32), w["rw"], n_valid=min(256, seq))
    x = _merge(x, w["gain"], w["wgate"], oa, ob.reshape(bsz * seq, R_WIDTH), w["wa"], w["wb"], w["wo"])
    last = lambda t: t.reshape(bsz, seq, -1)[:, -1]
    p_last = jnp.concatenate([last(rkv), last(lora)[:, :w["n_lora"]]], axis=-1)
    return x, k, v, _state_from_bd(s_bd), p_last


def _mixer_sample(x, page_table, cache_k4, cache_v4, layer, s0, p_prev, w):
    bs = x.shape[0]
    C = RWKV_CHUNK
    qb, kb, vb, k, v, rkv, lora = _proj(x, w["gain"], w["wqkv"], w["wrkv"], w["wlora"])
    qf = qb.astype(F32)
    kmean_s = _kmean_paged(page_table, cache_k4, layer)
    sel = _select(qf, kmean_s)[:, :MOBA_TOPK, :A_HEADS].transpose(0, 2, 1).reshape(bs, A_HEADS * MOBA_TOPK)
    oa = _decode_attn(page_table, sel, qf, k, v, cache_k4, cache_v4, layer).astype(BF16)
    pad = lambda t: jnp.concatenate([t[:, None, :], jnp.zeros((bs, C - 1, t.shape[1]), F32)], axis=1)
    prev_lora = jnp.zeros((bs, 1, LORA_PAD), F32).at[:, 0, :w["n_lora"]].set(p_prev[:, RKV_W:])
    ob, s_bd = _rwkv(pad(rkv), pad(lora), p_prev[:, None, :RKV_W], prev_lora, _state_to_bd(s0), w["rw"], n_valid=1)
    x = _merge(x, w["gain"], w["wgate"], oa, ob[:, 0], w["wa"], w["wb"], w["wo"])
    p_last = jnp.concatenate([rkv, lora[:, :w["n_lora"]]], axis=-1)
    return x, k, v, _state_from_bd(s_bd), p_last


def kernel(x_prompt, x_sample, cache_k, cache_v, state_wkv, state_shift, page_table, norm_attn, w_in, w_up_a, mu_shift, w0, w_dec2, a0, w_a2, w_g2, k_k, k_a, r_k, ln_x_w, ln_x_b, w_up_b, w_out, norm_ffn, ffn_w1, ffn_w3, ffn_w2, router, moe_w1, moe_w3, moe_w2, norm_final):
    bp, seq, d = x_prompt.shape
    bs, dec_seq, _ = x_sample.shape
    depth, n_pool, page = cache_k.shape[:3]
    assert dec_seq == 1 and 2 * page == MOBA_BLOCK and seq % MOBA_BLOCK == 0
    cache_k4 = cache_k.reshape(depth, n_pool, page, A_WIDTH)
    cache_v4 = cache_v.reshape(depth, n_pool, page, A_WIDTH)
    xp = x_prompt.reshape(bp * seq, d)
    xs = x_sample.reshape(bs, d)
    outs = [[] for _ in range(8)]
    for l in range(depth):
        w = _layer_weights(l, norm_attn, w_in, w_up_a, mu_shift, w0, w_dec2, a0, w_a2, w_g2, k_k, k_a, r_k, ln_x_w,
                           ln_x_b, w_up_b, w_out)
        xp, k_n, v_n, s_n, sh_n = _mixer_prompt(xp, bp, seq, w)
        for o, t in zip(outs[:4], (k_n.reshape(bp, seq, A_HEADS, HEAD_DIM), v_n.reshape(bp, seq, A_HEADS, HEAD_DIM), s_n, sh_n)):
            o.append(t)
        xs, k_n, v_n, s_n, sh_n = _mixer_sample(xs, page_table, cache_k4, cache_v4, l, state_wkv[l], state_shift[l], w)
        for o, t in zip(outs[4:], (k_n.reshape(bs, 1, A_HEADS, HEAD_DIM), v_n.reshape(bs, 1, A_HEADS, HEAD_DIM), s_n, sh_n)):
            o.append(t)
        gain = norm_ffn[l][None]
        i = l // 2
        if l % 2 == 0:
            w1, w3, w2 = ffn_w1[i].astype(BF16), ffn_w3[i].astype(BF16), ffn_w2[i].astype(BF16)
            xp = _ffn(xp, gain, w1, w3, w2)
            xs = _ffn(xs, gain, w1, w3, w2)
        else:
            w1, w3, w2 = moe_w1[i].astype(BF16), moe_w3[i].astype(BF16), moe_w2[i].astype(BF16)
            xp = _moe(xp, gain, router[i], w1, w3, w2, 256)
            xs = _moe(xs, gain, router[i], w1, w3, w2, 32)
    y_prompt = _final_norm(xp, norm_final[None]).reshape(bp, seq, d)
    y_sample = _final_norm(xs, norm_final[None]).reshape(bs, 1, d)
    return (y_prompt, y_sample) + tuple(jnp.stack(o) for o in outs)
```

```python
import functools

import jax
import jax.numpy as jnp
from jax import lax
from jax.experimental import pallas as pl
from jax.experimental.pallas import tpu as pltpu

F32 = jnp.float32
BF16 = jnp.bfloat16

A_HEADS = 8
HEAD_DIM = 64
A_WIDTH = 512
R_WIDTH = 512
MOBA_BLOCK = 256
MOBA_TOPK = 3
LORA_DECAY = 64
LORA_ICLR = 64
LORA_GATE = 160
LORA_PAD = 384
RKV_W = 3 * R_WIDTH
GN_EPS = 64e-5
RMS_EPS = 1e-6
MOE_TOPK = 2
RWKV_CHUNK = 64
NEG = -1e30
Q_SCALE = HEAD_DIM ** -0.5 * 1.4426950408889634
V_AUG = HEAD_DIM + 16
VMEM_LIMIT = 56 * 1024 * 1024


def _dot(a, b):
    return jnp.dot(a, b, preferred_element_type=F32)


def _dot_nt(a, b):
    return lax.dot_general(a, b, (((1,), (1,)), ((), ())), preferred_element_type=F32)


def _dot_tn(a, b):
    return lax.dot_general(a, b, (((0,), (0,)), ((), ())), preferred_element_type=F32)


def _mxu(a, b, precise):
    if precise:
        return jnp.dot(a.astype(F32), b.astype(F32), precision=lax.Precision.HIGHEST, preferred_element_type=F32)
    return jnp.dot(a.astype(BF16), b.astype(BF16), preferred_element_type=F32)


def _mxu_nt(a, b):
    return lax.dot_general(a, b, (((1,), (1,)), ((), ())), precision=lax.Precision.HIGHEST,
                           preferred_element_type=F32)


def _split2(x):
    hi = x.astype(BF16)
    lo = (x - hi.astype(F32)).astype(BF16)
    return hi, lo


def _split3(x):
    hi = x.astype(BF16)
    r1 = x - hi.astype(F32)
    mid = r1.astype(BF16)
    lo = (r1 - mid.astype(F32)).astype(BF16)
    return hi, mid, lo


def _dot_x2(x, m):
    hi, lo = _split2(x)
    return _dot(hi, m) + _dot(lo, m)


def _rms(x, gain):
    return x * lax.rsqrt(jnp.mean(x * x, axis=-1, keepdims=True) + RMS_EPS) * gain


def _cparams(sem):
    return pltpu.CompilerParams(dimension_semantics=sem, vmem_limit_bytes=VMEM_LIMIT)


def _proj_kernel(x_ref, gain_ref, wqkv_ref, wrkv_ref, wlora_ref,
                 q_ref, kb_ref, vb_ref, k_ref, v_ref, rkv_ref, lora_ref, *, precise):
    h = _rms(x_ref[...], gain_ref[...])
    h = h if precise else h.astype(BF16)
    q = _mxu(h, wqkv_ref[:, 0:A_WIDTH], precise)
    q_ref[...] = (q * Q_SCALE).astype(q_ref.dtype)
    k = _mxu(h, wqkv_ref[:, A_WIDTH:2 * A_WIDTH], precise)
    k_ref[...] = k
    kb_ref[...] = k.astype(kb_ref.dtype)
    v = _mxu(h, wqkv_ref[:, 2 * A_WIDTH:3 * A_WIDTH], precise)
    v_ref[...] = v
    vb_ref[...] = v.astype(vb_ref.dtype)
    rkv_ref[...] = _mxu(h, wrkv_ref[...], precise)
    lora_ref[...] = _mxu(h, wlora_ref[...], precise)


def _proj(x, gain, wqkv, wrkv, wlora, precise=False):
    n, d = x.shape
    tm = min(512, n)
    row = lambda w: pl.BlockSpec((tm, w), lambda i: (i, 0))
    full = lambda a: pl.BlockSpec(a.shape, lambda i: (0, 0))
    return pl.pallas_call(
        functools.partial(_proj_kernel, precise=precise),
        grid=(n // tm,),
        in_specs=[row(d), full(gain), full(wqkv), full(wrkv), full(wlora)],
        out_specs=[row(A_WIDTH)] * 5 + [row(RKV_W), row(LORA_PAD)],
        out_shape=[jax.ShapeDtypeStruct((n, A_WIDTH), F32 if precise else BF16)] * 3
        + [jax.ShapeDtypeStruct((n, A_WIDTH), F32)] * 2
        + [jax.ShapeDtypeStruct((n, RKV_W), F32), jax.ShapeDtypeStruct((n, LORA_PAD), F32)],
        compiler_params=_cparams(("parallel",)),
        name="proj",
    )(x, gain, wqkv, wrkv, wlora)


def _kmean_kernel(k_ref, o_ref):
    g = o_ref.shape[0]
    x = k_ref[...].reshape(g, MOBA_BLOCK, A_WIDTH)
    o_ref[...] = jnp.sum(x, axis=1) * (1.0 / MOBA_BLOCK)


def _kmean(k):
    n = k.shape[0]
    nblk = n // MOBA_BLOCK
    g = min(8, nblk)
    return pl.pallas_call(
        _kmean_kernel,
        grid=(nblk // g,),
        in_specs=[pl.BlockSpec((g * MOBA_BLOCK, A_WIDTH), lambda i: (i, 0))],
        out_specs=pl.BlockSpec((g, A_WIDTH), lambda i: (i, 0)),
        out_shape=jax.ShapeDtypeStruct((nblk, A_WIDTH), F32),
        compiler_params=_cparams(("parallel",)),
        name="kmean",
    )(k)


def _moba_kernel(qt_ref, k_ref, va_ref, km_ref, o_ref, bias_ref, s_scr, *, nb):
    i = pl.program_id(2)
    blk = MOBA_BLOCK
    qt = qt_ref[0]
    row = lax.broadcasted_iota(jnp.int32, qt.shape, 0)
    km_hi, km_lo = _split2(km_ref[0])
    bid = lax.broadcasted_iota(jnp.int32, (nb, blk), 0)
    qms = []
    for hh in range(2):
        qm = jnp.where((row >= hh * HEAD_DIM) & (row < (hh + 1) * HEAD_DIM), qt, jnp.zeros_like(qt))
        qms.append(qm)
        gate = _dot(km_hi, qm) + _dot(km_lo, qm)
        gate = jnp.where(bid < i, gate, -jnp.inf)
        rank = jnp.zeros((nb, blk), F32)
        for m in range(nb):
            gm = gate[m:m + 1, :]
            beats = (gm > gate) | ((gm == gate) & (bid > m))
            rank = rank + jnp.where(beats, 1.0, 0.0)
        sel = (bid < i) & (rank < float(MOBA_TOPK))
        bias_ref[hh] = jnp.where(sel, 0.0, NEG)

    kio = lax.broadcasted_iota(jnp.int32, (blk, blk), 0)
    qio = lax.broadcasted_iota(jnp.int32, (blk, blk), 1)
    causal = kio <= qio
    kd = k_ref[0, i]
    ms = []
    for hh in range(2):
        s = jnp.where(causal, _dot(kd, qms[hh]), NEG)
        s_scr[hh, nb] = s
        ms.append(jnp.max(s, axis=0, keepdims=True))
    n_pairs = (i + 1) // 2

    def scores(jj, ms):
        ms = list(ms)
        for u in range(2):
            j = 2 * jj + u
            kj = k_ref[0, j]
            for hh in range(2):
                s = _dot(kj, qms[hh]) + bias_ref[hh, pl.ds(j, 1), :]
                s_scr[hh, j] = s
                ms[hh] = jnp.maximum(ms[hh], jnp.max(s, axis=0, keepdims=True))
        return tuple(ms)

    ms = lax.fori_loop(0, n_pairs, scores, tuple(ms))

    vd = va_ref[0, i]
    accs = []
    for hh in range(2):
        p = jnp.exp2(s_scr[hh, nb] - ms[hh]).astype(BF16)
        accs.append(_dot(vd[hh * V_AUG:(hh + 1) * V_AUG, :], p))

    def weighted(jj, accs):
        accs = list(accs)
        for u in range(2):
            j = 2 * jj + u
            vj = va_ref[0, j]
            for hh in range(2):
                p = jnp.exp2(s_scr[hh, j] - ms[hh]).astype(BF16)
                accs[hh] = accs[hh] + _dot(vj[hh * V_AUG:(hh + 1) * V_AUG, :], p)
        return tuple(accs)

    accs = lax.fori_loop(0, n_pairs, weighted, tuple(accs))
    o = jnp.concatenate([a[0:HEAD_DIM] / a[HEAD_DIM:HEAD_DIM + 1] for a in accs], axis=0)
    o_ref[0] = o.T.astype(BF16)


def _moba_prompt(qb, kb, vb, kmean, bsz, seq):
    nb = seq // MOBA_BLOCK
    qt = qb.reshape(bsz, seq, A_WIDTH).transpose(0, 2, 1)
    k4 = kb.reshape(bsz, nb, MOBA_BLOCK, A_WIDTH)
    v5 = vb.reshape(bsz, nb, MOBA_BLOCK, A_HEADS, HEAD_DIM)
    ones = jnp.ones((bsz, nb, MOBA_BLOCK, A_HEADS, V_AUG - HEAD_DIM), BF16)
    va = jnp.concatenate([v5, ones], axis=-1).transpose(0, 1, 3, 4, 2).reshape(bsz, nb, A_HEADS * V_AUG, MOBA_BLOCK)
    km = kmean.reshape(bsz, nb, A_WIDTH)
    out = pl.pallas_call(
        functools.partial(_moba_kernel, nb=nb),
        grid=(bsz, A_HEADS // 2, nb),
        in_specs=[
            pl.BlockSpec((1, 128, MOBA_BLOCK), lambda b, h, i: (b, h, i)),
            pl.BlockSpec((1, nb, MOBA_BLOCK, 128), lambda b, h, i: (b, 0, 0, h)),
            pl.BlockSpec((1, nb, 2 * V_AUG, MOBA_BLOCK), lambda b, h, i: (b, 0, h, 0)),
            pl.BlockSpec((1, nb, 128), lambda b, h, i: (b, 0, h)),
        ],
        out_specs=pl.BlockSpec((1, MOBA_BLOCK, 128), lambda b, h, i: (b, i, h)),
        out_shape=jax.ShapeDtypeStruct((bsz, seq, A_WIDTH), BF16),
        scratch_shapes=[pltpu.VMEM((2, nb, MOBA_BLOCK), F32), pltpu.VMEM((2, nb + 1, MOBA_BLOCK, MOBA_BLOCK), F32)],
        compiler_params=_cparams(("parallel", "parallel", "arbitrary")),
        name="moba_prompt",
    )(qt, k4, va, km)
    return out.reshape(bsz * seq, A_WIDTH)


def _kmean_paged_kernel(pt_ref, ck_ref, o_ref, buf, sem, *, layer, n_pages):
    b = pl.program_id(0)
    for p in range(n_pages):
        pltpu.make_async_copy(ck_ref.at[layer, pt_ref[b, p]], buf.at[p], sem.at[p]).start()
    nb = n_pages // 2
    lane = lax.broadcasted_iota(jnp.int32, (A_WIDTH, nb), 1)
    res = jnp.zeros((A_WIDTH, nb), F32)
    for n in range(nb):
        for e in range(2):
            pltpu.make_async_copy(ck_ref.at[layer, 0], buf.at[2 * n + e], sem.at[2 * n + e]).wait()
        tot = jnp.sum(buf[2 * n] + buf[2 * n + 1], axis=-1, keepdims=True)
        res = jnp.where(lane == n, tot * (1.0 / MOBA_BLOCK), res)
    o_ref[0] = res


def _kmean_paged(page_table, cache_kt, layer):
    bs, n_pages = page_table.shape
    page = cache_kt.shape[3]
    nb = n_pages // 2
    return pl.pallas_call(
        functools.partial(_kmean_paged_kernel, layer=layer, n_pages=n_pages),
        grid_spec=pltpu.PrefetchScalarGridSpec(
            num_scalar_prefetch=1,
            grid=(bs,),
            in_specs=[pl.BlockSpec(memory_space=pl.ANY)],
            out_specs=pl.BlockSpec((1, A_WIDTH, nb), lambda b, pt: (b, 0, 0)),
            scratch_shapes=[pltpu.VMEM((n_pages, A_WIDTH, page), F32), pltpu.SemaphoreType.DMA((n_pages,))],
        ),
        out_shape=jax.ShapeDtypeStruct((bs, A_WIDTH, nb), F32),
        compiler_params=_cparams(("arbitrary",)),
        name="kmean_paged",
    )(page_table, cache_kt)


def _select_kernel(q_ref, km_ref, o_ref, *, bs, nb):
    head = lax.broadcasted_iota(jnp.int32, (A_HEADS, A_WIDTH), 0)
    chan = lax.broadcasted_iota(jnp.int32, (A_HEADS, A_WIDTH), 1)
    own = (chan >= head * HEAD_DIM) & (chan < (head + 1) * HEAD_DIM)
    bid = lax.broadcasted_iota(jnp.int32, (A_HEADS, nb), 1)
    bidf = bid.astype(F32)
    lane = lax.broadcasted_iota(jnp.int32, (A_HEADS, 128), 1)

    def body(b, _):
        qm = jnp.where(own, q_ref[pl.ds(b, 1), :], 0.0)
        gate = _mxu(qm, km_ref[b], True)
        rank = jnp.zeros((A_HEADS, nb), F32)
        for m in range(nb):
            gm = gate[:, m:m + 1]
            beats = (gm > gate) | ((gm == gate) & (bid > m))
            rank = rank + jnp.where(beats, 1.0, 0.0)
        res = jnp.zeros((A_HEADS, 128), F32)
        for r in range(MOBA_TOPK):
            idx = jnp.sum(jnp.where(rank == float(r), bidf, 0.0), axis=-1, keepdims=True)
            res = jnp.where(lane == r, idx, res)
        o_ref[b] = res.astype(jnp.int32)
        return 0

    lax.fori_loop(0, bs, body, 0)


def _select(qf, kmean_t):
    bs, _, nb = kmean_t.shape
    return pl.pallas_call(
        functools.partial(_select_kernel, bs=bs, nb=nb),
        out_shape=jax.ShapeDtypeStruct((bs, A_HEADS, 128), jnp.int32),
        compiler_params=pltpu.CompilerParams(vmem_limit_bytes=VMEM_LIMIT),
        name="moba_select",
    )(qf, kmean_t)


def _decode_kernel(pt_ref, sel_ref, q_ref, kn_ref, vn_ref, ck_ref, cv_ref, o_ref, kbuf, vbuf, sem, *, layer, pg):
    b = pl.program_id(0)
    copies = []
    for h in range(A_HEADS):
        for r in range(MOBA_TOPK):
            blk = sel_ref[b, h * MOBA_TOPK + r]
            for e in range(2):
                page = pt_ref[b, 2 * blk + e]
                for src, dst, s in ((ck_ref, kbuf, 0), (cv_ref, vbuf, 1)):
                    c = pltpu.make_async_copy(src.at[layer, page, pl.ds(h * HEAD_DIM, HEAD_DIM), :],
                                              dst.at[h, :, pl.ds((2 * r + e) * pg, pg)], sem.at[s])
                    c.start()
                    copies.append(c)
    for c in copies:
        c.wait()
    q = q_ref[0]
    kn = kn_ref[0]
    vn = vn_ref[0]
    for h in range(A_HEADS):
        s = _mxu(q, kbuf[h], True)[h:h + 1]
        s_self = jnp.sum(q[h:h + 1] * kn[h:h + 1], axis=-1, keepdims=True)
        m = jnp.maximum(jnp.max(s, axis=-1, keepdims=True), s_self)
        p = jnp.exp2(s - m)
        p_self = jnp.exp2(s_self - m)
        l = jnp.sum(p, axis=-1, keepdims=True) + p_self
        pv = _mxu_nt(jnp.broadcast_to(p, (8, p.shape[1])), vbuf[h])[0:1]
        o_ref[0, h:h + 1, :] = (pv + p_self * vn[h:h + 1]) / l


def _decode_attn(page_table, sel, qf, k_new, v_new, cache_kt, cache_vt, layer):
    bs = qf.shape[0]
    pg = cache_kt.shape[3]
    heads = lambda t: t.reshape(bs, A_HEADS, HEAD_DIM)
    row = pl.BlockSpec((1, A_HEADS, HEAD_DIM), lambda b, pt, sl: (b, 0, 0))
    gathered = pltpu.VMEM((A_HEADS, HEAD_DIM, 2 * MOBA_TOPK * pg), F32)
    out = pl.pallas_call(
        functools.partial(_decode_kernel, layer=layer, pg=pg),
        grid_spec=pltpu.PrefetchScalarGridSpec(
            num_scalar_prefetch=2,
            grid=(bs,),
            in_specs=[row, row, row, pl.BlockSpec(memory_space=pl.ANY), pl.BlockSpec(memory_space=pl.ANY)],
            out_specs=row,
            scratch_shapes=[gathered, gathered, pltpu.SemaphoreType.DMA((2,))],
        ),
        out_shape=jax.ShapeDtypeStruct((bs, A_HEADS, HEAD_DIM), F32),
        compiler_params=_cparams(("arbitrary",)),
        name="moba_decode",
    )(page_table, sel, heads(qf), heads(k_new), heads(v_new), cache_kt, cache_vt)
    return out.reshape(bs, A_WIDTH)


def _rwkv_kernel(rkv_ref, lora_ref, prkv_ref, plora_ref, s0_ref, mu_rkv_ref, mu_lora_ref, w0_ref, wdec_ref,
                 a0_ref, wa_ref, wg_ref, kk_ref, ka_ref, rk_ref, lnw_ref, lnb_ref, g_ref, tri_ref,
                 ob_ref, s_ref, c_rkv, c_lora, s_an, s_r, s_b, s_k, s_v, s_ld, s_y, *, T, C):
    i = pl.program_id(1)

    @pl.when(i == 0)
    def _():
        c_rkv[...] = prkv_ref[0]
        c_lora[...] = plora_ref[0]
        s_ref[0] = s0_ref[0]

    x = rkv_ref[0]
    xl = lora_ref[0]
    rowi = lax.broadcasted_iota(jnp.int32, (T, 1), 0)

    def shifted(cur, prev):
        return jnp.where(rowi == 0, prev, pltpu.roll(cur, 1, 0))

    xs = shifted(x, c_rkv[...])
    xls = shifted(xl, c_lora[...])
    c_rkv[...] = x[T - 1:T]
    c_lora[...] = xl[T - 1:T]
    pm = x + (xs - x) * mu_rkv_ref[...]
    pml = xl + (xls - xl) * mu_lora_ref[...]
    r = pm[:, 0:R_WIDTH]
    k = pm[:, R_WIDTH:2 * R_WIDTH]
    v = pm[:, 2 * R_WIDTH:3 * R_WIDTH]
    t01 = pml[:, 0:128]
    dec_arg = w0_ref[...] + _dot(jnp.tanh(t01).astype(BF16), wdec_ref[...])
    sp = jnp.maximum(-dec_arg, 0.0) + jnp.log(1.0 + jnp.exp(-jnp.abs(dec_arg)))
    ld = -jnp.exp(-sp - 0.5)
    a = jax.nn.sigmoid(a0_ref[...] + _dot(t01.astype(BF16), wa_ref[...]))
    g = _dot(jax.nn.sigmoid(pml[:, 128:LORA_PAD]).astype(BF16), wg_ref[...])
    gmat = g_ref[...]
    kk0 = k * kk_ref[...]
    kk = kk0 * lax.rsqrt(jnp.maximum(_dot_x2(kk0 * kk0, gmat), 1e-24))
    kmod = k * (1.0 + (a - 1.0) * ka_ref[...])
    bb = kk * a
    s_an[...] = -kk
    s_r[...] = r
    s_b[...] = bb
    s_k[...] = kmod
    s_v[...] = v
    s_ld[...] = ld

    tri = tri_ref[...]
    lane = lax.broadcasted_iota(jnp.int32, (C, 128), 1)
    in_h0 = lane < HEAD_DIM
    ri = lax.broadcasted_iota(jnp.int32, (2 * C, 2 * C), 0)
    ci = lax.broadcasted_iota(jnp.int32, (2 * C, 2 * C), 1)
    same = (ri >= C) == (ci >= C)
    strict = same & (ri > ci)
    incl = same & (ri >= ci)
    eye = jnp.where(ri == ci, 1.0, 0.0)

    def stack_masked(t):
        return jnp.concatenate([jnp.where(in_h0, t, 0.0), jnp.where(in_h0, 0.0, t)], axis=0)

    def chunk(c, _):
        rs = pl.ds(pl.multiple_of(c * C, C), C)
        ld_c = s_ld[rs, :]
        ld_hi, ld_mid, ld_lo = _split3(ld_c)
        cum = _dot(tri, ld_hi) + _dot(tri, ld_mid) + _dot(tri, ld_lo)
        cum_last = cum[C - 1:C]
        e_in = jnp.exp(cum)
        e_ex = jnp.exp(cum - ld_c)
        e_neg = jnp.exp(-cum)
        e_end = jnp.exp(cum_last - cum)
        g_end = jnp.exp(cum_last)
        at = s_an[rs, :] * e_ex
        rt = s_r[rs, :] * e_in
        bt = s_b[rs, :] * e_neg
        kt = s_k[rs, :] * e_neg
        bg = s_b[rs, :] * e_end
        kg = s_k[rs, :] * e_end
        vv = s_v[rs, :]
        pairs = range(4)
        lss = [slice(p * 128, (p + 1) * 128) for p in pairs]
        bf = lambda xs: [x.astype(BF16) for x in xs]
        ar_b = bf([jnp.concatenate([stack_masked(at[:, ls]), stack_masked(rt[:, ls])], axis=0) for ls in lss])
        bk_b = bf([jnp.concatenate([bt[:, ls], bt[:, ls], kt[:, ls], kt[:, ls]], axis=0) for ls in lss])
        vv_b = bf([stack_masked(vv[:, ls]) for ls in lss])
        bkg_b = bf([jnp.concatenate([stack_masked(bg[:, ls]), stack_masked(kg[:, ls])], axis=0) for ls in lss])
        big = [_dot_nt(a, b) for a, b in zip(ar_b, bk_b)]
        a_ab = [jnp.where(strict, m[0:2 * C, 0:2 * C], 0.0) for m in big]
        a_ak = bf([jnp.where(strict, m[0:2 * C, 2 * C:4 * C], 0.0) for m in big])
        a_rbk = bf([jnp.concatenate([jnp.where(incl, m[2 * C:4 * C, 0:2 * C], 0.0),
                                     jnp.where(incl, m[2 * C:4 * C, 2 * C:4 * C], 0.0)], axis=1) for m in big])
        inv = [eye + a for a in a_ab]
        pw = a_ab
        for _ in range(5):
            pw_b = bf(pw)
            pw = [_dot(x, x) for x in pw_b]
            inv = [iv + _dot(x.astype(BF16), iv.astype(BF16)) for x, iv in zip(pw, inv)]
        st = [s_ref[0, p] for p in pairs]
        from_state = [_dot_nt(a, s.astype(BF16)) for a, s in zip(ar_b, st)]
        rhs = [fs[0:2 * C] + _dot(a, v) for fs, a, v in zip(from_state, a_ak, vv_b)]
        u_b = bf([_dot(iv.astype(BF16), r.astype(BF16)) for iv, r in zip(inv, rhs)])
        uv_b = [jnp.concatenate([u, v], axis=0) for u, v in zip(u_b, vv_b)]
        y_s = [fs[2 * C:4 * C] + _dot(a, uv) for fs, a, uv in zip(from_state, a_rbk, uv_b)]
        for p in pairs:
            s_y[rs, lss[p]] = y_s[p][0:C] + y_s[p][C:2 * C]
            s_ref[0, p] = st[p] * g_end[:, lss[p]] + _dot_tn(uv_b[p], bkg_b[p])
        return 0

    lax.fori_loop(0, T // C, chunk, 0)

    y = s_y[...]
    mean = _dot_x2(y, gmat) * (1.0 / HEAD_DIM)
    d = y - mean
    var = _dot_x2(d * d, gmat) * (1.0 / HEAD_DIM)
    yn = d * lax.rsqrt(var + GN_EPS) * lnw_ref[...] + lnb_ref[...]
    bonus = _dot_x2(r * kmod * rk_ref[...], gmat) * v
    ob_ref[0] = ((yn + bonus) * g).astype(BF16)


def _rwkv(rkv, lora, prev_rkv, prev_lora, s0_bd, wts):
    bsz, seq, _ = rkv.shape
    C = RWKV_CHUNK
    T = min(256, seq)
    head_of = jnp.arange(R_WIDTH, dtype=jnp.int32) // HEAD_DIM
    gmat = (head_of[:, None] == head_of[None, :]).astype(BF16)
    tri = (jnp.arange(C)[:, None] >= jnp.arange(C)[None, :]).astype(BF16)
    consts = list(wts) + [gmat, tri]
    seq_spec = lambda w: pl.BlockSpec((1, T, w), lambda b, i: (b, i, 0))
    per_b = lambda a: pl.BlockSpec((1,) + a.shape[1:], lambda b, i: (b,) + (0,) * (a.ndim - 1))
    full = lambda a: pl.BlockSpec(a.shape, lambda b, i: (0,) * a.ndim)
    big = lambda: pltpu.VMEM((T, R_WIDTH), F32)
    return pl.pallas_call(
        functools.partial(_rwkv_kernel, T=T, C=C),
        grid=(bsz, seq // T),
        in_specs=[seq_spec(RKV_W), seq_spec(LORA_PAD), per_b(prev_rkv), per_b(prev_lora), per_b(s0_bd)]
        + [full(a) for a in consts],
        out_specs=[seq_spec(R_WIDTH), per_b(s0_bd)],
        out_shape=[jax.ShapeDtypeStruct((bsz, seq, R_WIDTH), BF16), jax.ShapeDtypeStruct(s0_bd.shape, F32)],
        scratch_shapes=[pltpu.VMEM((1, RKV_W), F32), pltpu.VMEM((1, LORA_PAD), F32)] + [big() for _ in range(7)],
        compiler_params=_cparams(("parallel", "arbitrary")),
        name="rwkv",
    )(rkv, lora, prev_rkv, prev_lora, s0_bd, *consts)


def _rwkv_step_kernel(rkv_ref, lora_ref, prkv_ref, plora_ref, s0_ref, mu_rkv_ref, mu_lora_ref, w0_ref, wdec_ref,
                      a0_ref, wa_ref, wg_ref, kk_ref, ka_ref, rk_ref, lnw_ref, lnb_ref,
                      ob_ref, s_ref, p_r, p_w, p_k, p_v, p_kk, p_a, p_g):
    b = pl.program_id(0)

    @pl.when(b == 0)
    def _():
        x = rkv_ref[...]
        xl = lora_ref[...]
        pm = x + (prkv_ref[...] - x) * mu_rkv_ref[...]
        pml = xl + (plora_ref[...] - xl) * mu_lora_ref[...]
        k = pm[:, R_WIDTH:2 * R_WIDTH]
        t01 = pml[:, 0:128]
        dec_arg = w0_ref[...] + _mxu(jnp.tanh(t01), wdec_ref[...], True)
        sp = jnp.maximum(-dec_arg, 0.0) + jnp.log(1.0 + jnp.exp(-jnp.abs(dec_arg)))
        a = jax.nn.sigmoid(a0_ref[...] + _mxu(t01, wa_ref[...], True))
        p_r[...] = pm[:, 0:R_WIDTH]
        p_w[...] = jnp.exp(-jnp.exp(-sp - 0.5))
        p_k[...] = k * (1.0 + (a - 1.0) * ka_ref[...])
        p_v[...] = pm[:, 2 * R_WIDTH:3 * R_WIDTH]
        p_kk[...] = k * kk_ref[...]
        p_a[...] = a
        p_g[...] = _mxu(jax.nn.sigmoid(pml[:, 128:LORA_PAD]), wg_ref[...], True)
        ob_ref[...] = jnp.zeros(ob_ref.shape, F32)

    rows = pl.ds(pl.multiple_of((b // 8) * 8, 8), 8)
    mine = lax.broadcasted_iota(jnp.int32, (8, 128), 0) == b % 8
    ri = lax.broadcasted_iota(jnp.int32, (128, 128), 0)
    ci = lax.broadcasted_iota(jnp.int32, (128, 128), 1)
    same_head = (ri >= HEAD_DIM) == (ci >= HEAD_DIM)
    eye = jnp.where(ri == ci, 1.0, 0.0)
    top = lax.broadcasted_iota(jnp.int32, (128, 1), 0) < HEAD_DIM
    left = lax.broadcasted_iota(jnp.int32, (1, 128), 1) < HEAD_DIM

    def per_head_rows(t):
        s0 = jnp.sum(jnp.where(left, t, 0.0), axis=-1, keepdims=True)
        s1 = jnp.sum(jnp.where(left, 0.0, t), axis=-1, keepdims=True)
        return jnp.where(left, s0, s1)

    def per_head_cols(t):
        s0 = jnp.sum(jnp.where(top, t, 0.0), axis=0, keepdims=True)
        s1 = jnp.sum(jnp.where(top, 0.0, t), axis=0, keepdims=True)
        return jnp.where(top, s0, s1)

    for p in range(4):
        ls = slice(p * 128, (p + 1) * 128)
        r, w, k, v, kk0, a, g = (jnp.sum(jnp.where(mine, t[rows, ls], 0.0), axis=0, keepdims=True)
                                 for t in (p_r, p_w, p_k, p_v, p_kk, p_a, p_g))
        kk = kk0 * lax.rsqrt(jnp.maximum(per_head_rows(kk0 * kk0), 1e-24))
        st = s0_ref[0, p]
        sa = jnp.sum(st * kk, axis=-1, keepdims=True)
        v_col = jnp.sum(eye * v, axis=-1, keepdims=True)
        st = st * w + jnp.where(same_head, v_col * k - sa * (kk * a), 0.0)
        s_ref[0, p] = st
        y = jnp.sum(st * r, axis=-1, keepdims=True)
        d = y - per_head_cols(y) * (1.0 / HEAD_DIM)
        yn = d * lax.rsqrt(per_head_cols(d * d) * (1.0 / HEAD_DIM) + GN_EPS)
        yn_row = jnp.sum(eye * yn, axis=0, keepdims=True)
        bonus = per_head_rows(r * k * rk_ref[:, ls]) * v
        out = (yn_row * lnw_ref[:, ls] + lnb_ref[:, ls] + bonus) * g
        ob_ref[rows, ls] = jnp.where(mine, out, ob_ref[rows, ls])


def _rwkv_step(rkv, lora, prev_rkv, prev_lora, s0_bd, wts):
    bs = rkv.shape[0]
    full = lambda a: pl.BlockSpec(a.shape, lambda b: (0,) * a.ndim)
    state = pl.BlockSpec((1,) + s0_bd.shape[1:], lambda b: (b, 0, 0, 0))
    args = [rkv, lora, prev_rkv, prev_lora]
    return pl.pallas_call(
        _rwkv_step_kernel,
        grid=(bs,),
        in_specs=[full(a) for a in args] + [state] + [full(a) for a in wts],
        out_specs=[pl.BlockSpec((bs, R_WIDTH), lambda b: (0, 0)), state],
        out_shape=[jax.ShapeDtypeStruct((bs, R_WIDTH), F32), jax.ShapeDtypeStruct(s0_bd.shape, F32)],
        scratch_shapes=[pltpu.VMEM((bs, R_WIDTH), F32) for _ in range(7)],
        compiler_params=_cparams(("arbitrary",)),
        name="rwkv_step",
    )(*args, s0_bd, *wts)


def _state_to_bd(s):
    b = s.shape[0]
    s = s.reshape(b, 4, 2, HEAD_DIM, HEAD_DIM)
    z = jnp.zeros_like(s[:, :, 0])
    top = jnp.concatenate([s[:, :, 0], z], axis=-1)
    bot = jnp.concatenate([z, s[:, :, 1]], axis=-1)
    return jnp.concatenate([top, bot], axis=-2)


def _state_from_bd(s):
    b = s.shape[0]
    h0 = s[:, :, :HEAD_DIM, :HEAD_DIM]
    h1 = s[:, :, HEAD_DIM:, HEAD_DIM:]
    return jnp.stack([h0, h1], axis=2).reshape(b, 8, HEAD_DIM, HEAD_DIM)


def _merge_kernel(x_ref, gain_ref, wg_ref, oa_ref, ob_ref, wa_ref, wb_ref, wo_ref, o_ref, *, precise):
    x = x_ref[...]
    d = x.shape[1]
    gates = jax.nn.sigmoid(_mxu(_rms(x, gain_ref[...]), wg_ref[...], precise))
    ya = _mxu(oa_ref[...], wa_ref[...], precise)
    yb = _mxu(ob_ref[...], wb_ref[...], precise)
    merged = gates[:, :d] * ya + gates[:, d:] * yb
    o_ref[...] = x + _mxu(merged, wo_ref[...], precise)


def _merge(x, gain, wg, oa, ob, wa, wb, wo, precise=False):
    n, d = x.shape
    tm = min(512, n)
    row = lambda w: pl.BlockSpec((tm, w), lambda i: (i, 0))
    full = lambda a: pl.BlockSpec(a.shape, lambda i: (0, 0))
    return pl.pallas_call(
        functools.partial(_merge_kernel, precise=precise),
        grid=(n // tm,),
        in_specs=[row(d), full(gain), full(wg), row(A_WIDTH), row(R_WIDTH), full(wa), full(wb), full(wo)],
        out_specs=row(d),
        out_shape=jax.ShapeDtypeStruct((n, d), F32),
        compiler_params=_cparams(("parallel",)),
        name="merge",
    )(x, gain, wg, oa, ob, wa, wb, wo)


def _ffn_kernel(x_ref, gain_ref, w1_ref, w3_ref, w2_ref, o_ref, h_scr, acc, *, precise):
    f = pl.program_id(1)

    @pl.when(f == 0)
    def _():
        h_scr[...] = _rms(x_ref[...], gain_ref[...]).astype(h_scr.dtype)
        acc[...] = x_ref[...]

    h = h_scr[...]
    t = jax.nn.silu(_mxu(h, w1_ref[...], precise)) * _mxu(h, w3_ref[...], precise)
    acc[...] += _mxu(t, w2_ref[...], precise)

    @pl.when(f == pl.num_programs(1) - 1)
    def _():
        o_ref[...] = acc[...]


def _ffn(x, gain, w1, w3, w2, precise=False):
    n, d = x.shape
    dff = w1.shape[1]
    tm = min(512, n)
    tf = dff // 2
    return pl.pallas_call(
        functools.partial(_ffn_kernel, precise=precise),
        grid=(n // tm, dff // tf),
        in_specs=[pl.BlockSpec((tm, d), lambda i, f: (i, 0)), pl.BlockSpec((1, d), lambda i, f: (0, 0)),
                  pl.BlockSpec((d, tf), lambda i, f: (0, f)), pl.BlockSpec((d, tf), lambda i, f: (0, f)),
                  pl.BlockSpec((tf, d), lambda i, f: (f, 0))],
        out_specs=pl.BlockSpec((tm, d), lambda i, f: (i, 0)),
        out_shape=jax.ShapeDtypeStruct((n, d), F32),
        scratch_shapes=[pltpu.VMEM((tm, d), F32 if precise else BF16), pltpu.VMEM((tm, d), F32)],
        compiler_params=_cparams(("parallel", "arbitrary")),
        name="ffn",
    )(x, gain, w1, w3, w2)


def _router_kernel(x_ref, gain_ref, wr_ref, h_ref, o_ref, *, n_exp):
    h = _rms(x_ref[...], gain_ref[...])
    h_ref[...] = h.astype(h_ref.dtype)
    logits = _mxu(h, wr_ref[...], True)
    lane = lax.broadcasted_iota(jnp.int32, logits.shape, 1)
    lanef = lane.astype(F32)
    logits = jnp.where(lane < n_exp, logits, -jnp.inf)
    m1 = jnp.max(logits, axis=-1, keepdims=True)
    e1 = jnp.min(jnp.where(logits == m1, lanef, 1e9), axis=-1, keepdims=True)
    rest = jnp.where(lanef == e1, -jnp.inf, logits)
    m2 = jnp.max(rest, axis=-1, keepdims=True)
    e2 = jnp.min(jnp.where(rest == m2, lanef, 1e9), axis=-1, keepdims=True)
    z = jnp.exp(m2 - m1)
    g1 = 1.0 / (1.0 + z)
    g2 = z / (1.0 + z)
    res = jnp.where(lane == 0, e1, jnp.where(lane == 1, e2, jnp.where(lane == 2, g1, jnp.where(lane == 3, g2, 0.0))))
    o_ref[...] = res[:, 0:8]


def _router(x, gain, w_router, precise):
    n, d = x.shape
    n_exp = w_router.shape[1]
    tm = min(512, n)
    wp = jnp.zeros((d, 128), F32).at[:, :n_exp].set(w_router)
    return pl.pallas_call(
        functools.partial(_router_kernel, n_exp=n_exp),
        grid=(n // tm,),
        in_specs=[pl.BlockSpec((tm, d), lambda i: (i, 0)), pl.BlockSpec((1, d), lambda i: (0, 0)),
                  pl.BlockSpec((d, 128), lambda i: (0, 0))],
        out_specs=[pl.BlockSpec((tm, d), lambda i: (i, 0)), pl.BlockSpec((tm, 8), lambda i: (i, 0))],
        out_shape=[jax.ShapeDtypeStruct((n, d), F32 if precise else BF16), jax.ShapeDtypeStruct((n, 8), F32)],
        compiler_params=_cparams(("parallel",)),
        name="router",
    )(x, gain, wp)


def _experts_kernel(be_ref, xb_ref, rg_ref, w1_ref, w3_ref, w2_ref, o_ref, *, precise):
    x = xb_ref[...]
    t = jax.nn.silu(_mxu(x, w1_ref[0], precise)) * _mxu(x, w3_ref[0], precise)
    o_ref[...] = _mxu(t, w2_ref[0], precise) * rg_ref[...]


def _experts(blk_e, xb, row_gate, w1, w3, w2, rb, precise):
    r, d = xb.shape
    dffe = w1.shape[2]
    return pl.pallas_call(
        functools.partial(_experts_kernel, precise=precise),
        grid_spec=pltpu.PrefetchScalarGridSpec(
            num_scalar_prefetch=1,
            grid=(r // rb,),
            in_specs=[pl.BlockSpec((rb, d), lambda i, be: (i, 0)), pl.BlockSpec((rb, 1), lambda i, be: (i, 0)),
                      pl.BlockSpec((1, d, dffe), lambda i, be: (be[i], 0, 0)),
                      pl.BlockSpec((1, d, dffe), lambda i, be: (be[i], 0, 0)),
                      pl.BlockSpec((1, dffe, d), lambda i, be: (be[i], 0, 0))],
            out_specs=pl.BlockSpec((rb, d), lambda i, be: (i, 0)),
        ),
        out_shape=jax.ShapeDtypeStruct((r, d), F32),
        compiler_params=_cparams(("arbitrary",)),
        name="experts",
    )(blk_e, xb, row_gate, w1, w3, w2)


def _moe(x, gain, w_router, w1, w3, w2, rb, precise=False):
    n, d = x.shape
    n_exp = w_router.shape[1]
    h, route = _router(x, gain, w_router, precise)
    top_e = route[:, 0:MOE_TOPK].astype(jnp.int32)
    gate = route[:, MOE_TOPK:2 * MOE_TOPK]
    na = n * MOE_TOPK
    e_flat = top_e.reshape(na)
    g_flat = gate.reshape(na)
    onehot = (e_flat[:, None] == jnp.arange(n_exp, dtype=jnp.int32)[None, :]).astype(jnp.int32)
    within = jnp.cumsum(onehot, axis=0) - onehot
    counts = jnp.sum(onehot, axis=0)
    padded = (counts + rb - 1) // rb * rb
    pend = jnp.cumsum(padded)
    pstart = pend - padded
    dest = pstart[e_flat] + jnp.sum(within * onehot, axis=1)
    r = (na + n_exp * (rb - 1) + rb - 1) // rb * rb
    tok = jnp.arange(na, dtype=jnp.int32) // MOE_TOPK
    row_tok = jnp.zeros((r,), jnp.int32).at[dest].set(tok)
    row_gate = jnp.zeros((r,), F32).at[dest].set(g_flat)
    blk_start = jnp.arange(r // rb, dtype=jnp.int32) * rb
    blk_e = jnp.minimum(jnp.searchsorted(pend, blk_start, side="right"), n_exp - 1).astype(jnp.int32)
    xb = h[row_tok]
    yb = _experts(blk_e, xb, row_gate[:, None], w1, w3, w2, rb, precise)
    pos = dest.reshape(n, MOE_TOPK)
    return x + yb[pos[:, 0]] + yb[pos[:, 1]]


def _norm_kernel(x_ref, gain_ref, o_ref):
    o_ref[...] = _rms(x_ref[...], gain_ref[...])


def _final_norm(x, gain):
    n, d = x.shape
    tm = min(1024, n)
    return pl.pallas_call(
        _norm_kernel,
        grid=(n // tm,),
        in_specs=[pl.BlockSpec((tm, d), lambda i: (i, 0)), pl.BlockSpec((1, d), lambda i: (0, 0))],
        out_specs=pl.BlockSpec((tm, d), lambda i: (i, 0)),
        out_shape=jax.ShapeDtypeStruct((n, d), F32),
        compiler_params=_cparams(("parallel",)),
        name="final_norm",
    )(x, gain)


MIXER_WEIGHTS = ("norm_attn", "w_in", "w_up_a", "mu_shift", "w0", "w_dec2", "a0", "w_a2", "w_g2", "k_k", "k_a", "r_k",
                 "ln_x_w", "ln_x_b", "w_up_b", "w_out")


def _layer_weights(l, dt, norm_attn, w_in, w_up_a, mu_shift, w0, w_dec2, a0, w_a2, w_g2, k_k, k_a, r_k, ln_x_w, ln_x_b,
                   w_up_b, w_out):
    d = w_in.shape[1]
    wi = w_in[l]
    c0 = 3 * A_WIDTH
    c1 = c0 + RKV_W
    n_lora = LORA_DECAY + LORA_ICLR + LORA_GATE
    c2 = c1 + n_lora
    wqkv = wi[:, :c0].astype(dt)
    wrkv = wi[:, c0:c1].astype(dt)
    wlora = jnp.zeros((d, LORA_PAD), dt).at[:, :n_lora].set(wi[:, c1:c2].astype(dt))
    wgate = wi[:, c2:].astype(dt)
    mu = mu_shift[l]
    mu_rkv = mu[:RKV_W][None]
    mu_lora = jnp.zeros((1, LORA_PAD), F32).at[0, :n_lora].set(mu[RKV_W:])
    wdec = jnp.zeros((128, R_WIDTH), dt).at[:LORA_DECAY].set(w_dec2[l].astype(dt))
    wa = jnp.zeros((128, R_WIDTH), dt).at[LORA_DECAY:LORA_DECAY + LORA_ICLR].set(w_a2[l].astype(dt))
    wg = jnp.zeros((LORA_PAD - 128, R_WIDTH), dt).at[:LORA_GATE].set(w_g2[l].astype(dt))
    rw = (mu_rkv, mu_lora, w0[l][None], wdec, a0[l][None], wa, wg, k_k[l][None], k_a[l][None],
          r_k[l].reshape(1, R_WIDTH), ln_x_w[l][None], ln_x_b[l][None])
    return dict(gain=norm_attn[l][None], wqkv=wqkv, wrkv=wrkv, wlora=wlora, wgate=wgate, rw=rw,
                wa=w_up_a[l].astype(dt), wb=w_up_b[l].astype(dt), wo=w_out[l].astype(dt), n_lora=n_lora)


def _mixer_prompt(x, bsz, seq, w):
    qb, kb, vb, k, v, rkv, lora = _proj(x, w["gain"], w["wqkv"], w["wrkv"], w["wlora"])
    oa = _moba_prompt(qb, kb, vb, _kmean(k), bsz, seq)
    zeros = lambda width: jnp.zeros((bsz, 1, width), F32)
    ob, s_bd = _rwkv(rkv.reshape(bsz, seq, RKV_W), lora.reshape(bsz, seq, LORA_PAD), zeros(RKV_W), zeros(LORA_PAD),
                     jnp.zeros((bsz, 4, 128, 128), F32), w["rw"])
    x = _merge(x, w["gain"], w["wgate"], oa, ob.reshape(bsz * seq, R_WIDTH), w["wa"], w["wb"], w["wo"])
    last = lambda t: t.reshape(bsz, seq, -1)[:, -1]
    p_last = jnp.concatenate([last(rkv), last(lora)[:, :w["n_lora"]]], axis=-1)
    return x, k, v, _state_from_bd(s_bd), p_last


def _mixer_sample(x, page_table, cache_kt, cache_vt, layer, s0, p_prev, w):
    bs = x.shape[0]
    q, _, _, k, v, rkv, lora = _proj(x, w["gain"], w["wqkv"], w["wrkv"], w["wlora"], precise=True)
    kmean_t = _kmean_paged(page_table, cache_kt, layer)
    sel = _select(q, kmean_t)[:, :, :MOBA_TOPK].reshape(bs, A_HEADS * MOBA_TOPK)
    oa = _decode_attn(page_table, sel, q, k, v, cache_kt, cache_vt, layer)
    prev_lora = jnp.zeros((bs, LORA_PAD), F32).at[:, :w["n_lora"]].set(p_prev[:, RKV_W:])
    ob, s_bd = _rwkv_step(rkv, lora, p_prev[:, :RKV_W], prev_lora, _state_to_bd(s0), w["rw"])
    x = _merge(x, w["gain"], w["wgate"], oa, ob, w["wa"], w["wb"], w["wo"], precise=True)
    p_last = jnp.concatenate([rkv, lora[:, :w["n_lora"]]], axis=-1)
    return x, k, v, _state_from_bd(s_bd), p_last


def _prompt_group(x_prompt, wts):
    bp, seq, d = x_prompt.shape
    x = x_prompt.reshape(bp * seq, d)
    outs = [[] for _ in range(4)]
    for l in range(wts["w_in"].shape[0]):
        w = _layer_weights(l, BF16, *(wts[n] for n in MIXER_WEIGHTS))
        x, k, v, s, sh = _mixer_prompt(x, bp, seq, w)
        for o, t in zip(outs, (k.reshape(bp, seq, A_HEADS, HEAD_DIM), v.reshape(bp, seq, A_HEADS, HEAD_DIM), s, sh)):
            o.append(t)
        gain = wts["norm_ffn"][l][None]
        i = l // 2
        bf = lambda t: t[i].astype(BF16)
        if l % 2 == 0:
            x = _ffn(x, gain, bf(wts["ffn_w1"]), bf(wts["ffn_w3"]), bf(wts["ffn_w2"]))
        else:
            x = _moe(x, gain, wts["router"][i], bf(wts["moe_w1"]), bf(wts["moe_w3"]), bf(wts["moe_w2"]), 256)
    y = _final_norm(x, wts["norm_final"][None]).reshape(bp, seq, d)
    return (y,) + tuple(jnp.stack(o) for o in outs)


def _sample_group(x_sample, cache_k, cache_v, state_wkv, state_shift, page_table, wts):
    bs, dec_seq, d = x_sample.shape
    depth, n_pool, page = cache_k.shape[:3]
    assert dec_seq == 1 and 2 * page == MOBA_BLOCK
    cache_kt = cache_k.transpose(0, 1, 3, 4, 2).reshape(depth, n_pool, A_WIDTH, page)
    cache_vt = cache_v.transpose(0, 1, 3, 4, 2).reshape(depth, n_pool, A_WIDTH, page)
    x = x_sample.reshape(bs, d)
    outs = [[] for _ in range(4)]
    for l in range(depth):
        w = _layer_weights(l, F32, *(wts[n] for n in MIXER_WEIGHTS))
        x, k, v, s, sh = _mixer_sample(x, page_table, cache_kt, cache_vt, l, state_wkv[l], state_shift[l], w)
        for o, t in zip(outs, (k.reshape(bs, 1, A_HEADS, HEAD_DIM), v.reshape(bs, 1, A_HEADS, HEAD_DIM), s, sh)):
            o.append(t)
        gain = wts["norm_ffn"][l][None]
        i = l // 2
        if l % 2 == 0:
            x = _ffn(x, gain, wts["ffn_w1"][i], wts["ffn_w3"][i], wts["ffn_w2"][i], precise=True)
        else:
            x = _moe(x, gain, wts["router"][i], wts["moe_w1"][i], wts["moe_w3"][i], wts["moe_w2"][i], 32, precise=True)
    y = _final_norm(x, wts["norm_final"][None]).reshape(bs, 1, d)
    return (y,) + tuple(jnp.stack(o) for o in outs)


def kernel(x_prompt, x_sample, cache_k, cache_v, state_wkv, state_shift, page_table, norm_attn, w_in, w_up_a, mu_shift, w0, w_dec2, a0, w_a2, w_g2, k_k, k_a, r_k, ln_x_w, ln_x_b, w_up_b, w_out, norm_ffn, ffn_w1, ffn_w3, ffn_w2, router, moe_w1, moe_w3, moe_w2, norm_final):
    assert x_prompt.shape[1] % MOBA_BLOCK == 0
    wts = dict(norm_attn=norm_attn, w_in=w_in, w_up_a=w_up_a, mu_shift=mu_shift, w0=w0, w_dec2=w_dec2, a0=a0, w_a2=w_a2,
               w_g2=w_g2, k_k=k_k, k_a=k_a, r_k=r_k, ln_x_w=ln_x_w, ln_x_b=ln_x_b, w_up_b=w_up_b, w_out=w_out,
               norm_ffn=norm_ffn, ffn_w1=ffn_w1, ffn_w3=ffn_w3, ffn_w2=ffn_w2, router=router, moe_w1=moe_w1,
               moe_w3=moe_w3, moe_w2=moe_w2, norm_final=norm_final)
    y_p, k_p, v_p, s_p, sh_p = _prompt_group(x_prompt, wts)
    y_s, k_s, v_s, s_s, sh_s = _sample_group(x_sample, cache_k, cache_v, state_wkv, state_shift, page_table, wts)
    return (y_p, y_s, k_p, v_p, s_p, sh_p, k_s, v_s, s_s, sh_s)
```

```python
import functools

import jax
import jax.numpy as jnp
from jax import lax
from jax.experimental import pallas as pl
from jax.experimental.pallas import tpu as pltpu

F32 = jnp.float32
BF16 = jnp.bfloat16

A_HEADS = 8
HEAD_DIM = 64
A_WIDTH = 512
R_WIDTH = 512
MOBA_BLOCK = 256
MOBA_TOPK = 3
LORA_DECAY = 64
LORA_ICLR = 64
LORA_GATE = 160
LORA_PAD = 384
RKV_W = 3 * R_WIDTH
GN_EPS = 64e-5
RMS_EPS = 1e-6
MOE_TOPK = 2
RWKV_CHUNK = 64
NEG = -1e30
Q_SCALE = HEAD_DIM ** -0.5 * 1.4426950408889634
V_AUG = HEAD_DIM + 16
VMEM_LIMIT = 56 * 1024 * 1024


def _dot(a, b):
    return jnp.dot(a, b, preferred_element_type=F32)


def _dot_nt(a, b):
    return lax.dot_general(a, b, (((1,), (1,)), ((), ())), preferred_element_type=F32)


def _dot_tn(a, b):
    return lax.dot_general(a, b, (((0,), (0,)), ((), ())), preferred_element_type=F32)


def _mxu(a, b, precise):
    if precise:
        return jnp.dot(a.astype(F32), b.astype(F32), precision=lax.Precision.HIGHEST, preferred_element_type=F32)
    return jnp.dot(a.astype(BF16), b.astype(BF16), preferred_element_type=F32)


def _mxu_nt(a, b):
    return lax.dot_general(a, b, (((1,), (1,)), ((), ())), precision=lax.Precision.HIGHEST,
                           preferred_element_type=F32)


def _split2(x):
    hi = x.astype(BF16)
    lo = (x - hi.astype(F32)).astype(BF16)
    return hi, lo


def _split3(x):
    hi = x.astype(BF16)
    r1 = x - hi.astype(F32)
    mid = r1.astype(BF16)
    lo = (r1 - mid.astype(F32)).astype(BF16)
    return hi, mid, lo


def _dot_x2(x, m):
    hi, lo = _split2(x)
    return _dot(hi, m) + _dot(lo, m)


def _rms(x, gain):
    return x * lax.rsqrt(jnp.mean(x * x, axis=-1, keepdims=True) + RMS_EPS) * gain


def _cparams(sem):
    return pltpu.CompilerParams(dimension_semantics=sem, vmem_limit_bytes=VMEM_LIMIT)


def _proj_kernel(x_ref, gain_ref, wqkv_ref, wrkv_ref, wlora_ref, q_ref, *outs, precise):
    k_ref, v_ref, rkv_ref, lora_ref = outs[-4:]
    h = _rms(x_ref[...], gain_ref[...])
    h = h if precise else h.astype(BF16)
    q = _mxu(h, wqkv_ref[:, 0:A_WIDTH], precise) * Q_SCALE
    k = _mxu(h, wqkv_ref[:, A_WIDTH:2 * A_WIDTH], precise)
    v = _mxu(h, wqkv_ref[:, 2 * A_WIDTH:3 * A_WIDTH], precise)
    k_ref[...] = k
    v_ref[...] = v
    rkv_ref[...] = _mxu(h, wrkv_ref[...], precise)
    lora_ref[...] = _mxu(h, wlora_ref[...], precise)
    if precise:
        q_ref[...] = q
        return
    kb_ref, vb_ref = outs[:2]
    q_ref[0] = q.T.astype(BF16)
    kb_ref[...] = k.astype(BF16)
    ones = jnp.ones((V_AUG - HEAD_DIM, MOBA_BLOCK), F32)
    for u in range(v.shape[0] // MOBA_BLOCK):
        vt = v[u * MOBA_BLOCK:(u + 1) * MOBA_BLOCK].T
        parts = []
        for hd in range(A_HEADS):
            parts += [vt[hd * HEAD_DIM:(hd + 1) * HEAD_DIM], ones]
        vb_ref[0, u] = jnp.concatenate(parts, axis=0).astype(BF16)


def _proj(x, gain, wqkv, wrkv, wlora, seq=None, precise=False):
    n, d = x.shape
    tm = min(512, n)
    row = lambda w: pl.BlockSpec((tm, w), lambda i: (i, 0))
    full = lambda a: pl.BlockSpec(a.shape, lambda i: (0, 0))
    if precise:
        attn_specs = [row(A_WIDTH)]
        attn_shapes = [jax.ShapeDtypeStruct((n, A_WIDTH), F32)]
    else:
        tps = seq // tm
        nbt = tm // MOBA_BLOCK
        attn_specs = [pl.BlockSpec((1, A_WIDTH, tm), lambda i: (i // tps, 0, i % tps)), row(A_WIDTH),
                      pl.BlockSpec((1, nbt, A_HEADS * V_AUG, MOBA_BLOCK), lambda i: (i // tps, i % tps, 0, 0))]
        attn_shapes = [jax.ShapeDtypeStruct((n // seq, A_WIDTH, seq), BF16), jax.ShapeDtypeStruct((n, A_WIDTH), BF16),
                       jax.ShapeDtypeStruct((n // seq, seq // MOBA_BLOCK, A_HEADS * V_AUG, MOBA_BLOCK), BF16)]
    return pl.pallas_call(
        functools.partial(_proj_kernel, precise=precise),
        grid=(n // tm,),
        in_specs=[row(d), full(gain), full(wqkv), full(wrkv), full(wlora)],
        out_specs=attn_specs + [row(A_WIDTH)] * 2 + [row(RKV_W), row(LORA_PAD)],
        out_shape=attn_shapes + [jax.ShapeDtypeStruct((n, A_WIDTH), F32)] * 2
        + [jax.ShapeDtypeStruct((n, RKV_W), F32), jax.ShapeDtypeStruct((n, LORA_PAD), F32)],
        compiler_params=_cparams(("parallel",)),
        name="proj",
    )(x, gain, wqkv, wrkv, wlora)


def _kmean_kernel(k_ref, o_ref):
    g = o_ref.shape[0]
    x = k_ref[...].reshape(g, MOBA_BLOCK, A_WIDTH)
    o_ref[...] = jnp.sum(x, axis=1) * (1.0 / MOBA_BLOCK)


def _kmean(k):
    n = k.shape[0]
    nblk = n // MOBA_BLOCK
    g = min(8, nblk)
    return pl.pallas_call(
        _kmean_kernel,
        grid=(nblk // g,),
        in_specs=[pl.BlockSpec((g * MOBA_BLOCK, A_WIDTH), lambda i: (i, 0))],
        out_specs=pl.BlockSpec((g, A_WIDTH), lambda i: (i, 0)),
        out_shape=jax.ShapeDtypeStruct((nblk, A_WIDTH), F32),
        compiler_params=_cparams(("parallel",)),
        name="kmean",
    )(k)


def _moba_kernel(qt_ref, k_ref, va_ref, km_ref, o_ref, bias_ref, s_scr, *, nb, unroll):
    i = pl.program_id(2)
    blk = MOBA_BLOCK
    qt = qt_ref[0]
    row = lax.broadcasted_iota(jnp.int32, qt.shape, 0)
    km_hi, km_lo = _split2(km_ref[0])
    bid = lax.broadcasted_iota(jnp.int32, (nb, blk), 0)
    qms = []
    for hh in range(2):
        qm = jnp.where((row >= hh * HEAD_DIM) & (row < (hh + 1) * HEAD_DIM), qt, jnp.zeros_like(qt))
        qms.append(qm)
        gate = _dot(km_hi, qm) + _dot(km_lo, qm)
        gate = jnp.where(bid < i, gate, -jnp.inf)
        rank = jnp.zeros((nb, blk), F32)
        for m in range(nb):
            gm = gate[m:m + 1, :]
            beats = (gm > gate) | ((gm == gate) & (bid > m))
            rank = rank + jnp.where(beats, 1.0, 0.0)
        sel = (bid < i) & (rank < float(MOBA_TOPK))
        bias_ref[hh] = jnp.where(sel, 0.0, NEG)

    kio = lax.broadcasted_iota(jnp.int32, (blk, blk), 0)
    qio = lax.broadcasted_iota(jnp.int32, (blk, blk), 1)
    causal = kio <= qio
    kd = k_ref[0, i]
    ms = []
    for hh in range(2):
        s = jnp.where(causal, _dot(kd, qms[hh]), NEG)
        s_scr[hh, nb] = s
        ms.append(jnp.max(s, axis=0, keepdims=True))
    n_trips = (i + unroll - 1) // unroll

    def scores(jj, ms):
        ms = list(ms)
        for u in range(unroll):
            j = unroll * jj + u
            kj = k_ref[0, j]
            for hh in range(2):
                s = _dot(kj, qms[hh]) + bias_ref[hh, pl.ds(j, 1), :]
                s_scr[hh, j] = s
                ms[hh] = jnp.maximum(ms[hh], jnp.max(s, axis=0, keepdims=True))
        return tuple(ms)

    ms = lax.fori_loop(0, n_trips, scores, tuple(ms))

    vd = va_ref[0, i]
    accs = []
    for hh in range(2):
        p = jnp.exp2(s_scr[hh, nb] - ms[hh]).astype(BF16)
        accs.append(_dot(vd[hh * V_AUG:(hh + 1) * V_AUG, :], p))

    def weighted(jj, accs):
        accs = list(accs)
        for u in range(unroll):
            j = unroll * jj + u
            vj = va_ref[0, j]
            for hh in range(2):
                p = jnp.exp2(s_scr[hh, j] - ms[hh]).astype(BF16)
                accs[hh] = accs[hh] + _dot(vj[hh * V_AUG:(hh + 1) * V_AUG, :], p)
        return tuple(accs)

    accs = lax.fori_loop(0, n_trips, weighted, tuple(accs))
    o = jnp.concatenate([a[0:HEAD_DIM] / a[HEAD_DIM:HEAD_DIM + 1] for a in accs], axis=0)
    o_ref[0] = o.T.astype(BF16)


def _moba_prompt(qt, kb, va, kmean, bsz, seq):
    nb = seq // MOBA_BLOCK
    k4 = kb.reshape(bsz, nb, MOBA_BLOCK, A_WIDTH)
    km = kmean.reshape(bsz, nb, A_WIDTH)
    out = pl.pallas_call(
        functools.partial(_moba_kernel, nb=nb, unroll=4 if nb % 4 == 0 else 2),
        grid=(bsz, A_HEADS // 2, nb),
        in_specs=[
            pl.BlockSpec((1, 128, MOBA_BLOCK), lambda b, h, i: (b, h, i)),
            pl.BlockSpec((1, nb, MOBA_BLOCK, 128), lambda b, h, i: (b, 0, 0, h)),
            pl.BlockSpec((1, nb, 2 * V_AUG, MOBA_BLOCK), lambda b, h, i: (b, 0, h, 0)),
            pl.BlockSpec((1, nb, 128), lambda b, h, i: (b, 0, h)),
        ],
        out_specs=pl.BlockSpec((1, MOBA_BLOCK, 128), lambda b, h, i: (b, i, h)),
        out_shape=jax.ShapeDtypeStruct((bsz, seq, A_WIDTH), BF16),
        scratch_shapes=[pltpu.VMEM((2, nb, MOBA_BLOCK), F32), pltpu.VMEM((2, nb + 1, MOBA_BLOCK, MOBA_BLOCK), F32)],
        compiler_params=_cparams(("parallel", "parallel", "arbitrary")),
        name="moba_prompt",
    )(qt, k4, va, km)
    return out.reshape(bsz * seq, A_WIDTH)


def _kmean_paged_kernel(pt_ref, ck_ref, o_ref, buf, sem, *, layer, n_pages):
    b = pl.program_id(0)
    for p in range(n_pages):
        pltpu.make_async_copy(ck_ref.at[layer, pt_ref[b, p]], buf.at[p], sem.at[p]).start()
    nb = n_pages // 2
    lane = lax.broadcasted_iota(jnp.int32, (A_WIDTH, nb), 1)
    res = jnp.zeros((A_WIDTH, nb), F32)
    for n in range(nb):
        for e in range(2):
            pltpu.make_async_copy(ck_ref.at[layer, 0], buf.at[2 * n + e], sem.at[2 * n + e]).wait()
        tot = jnp.sum(buf[2 * n] + buf[2 * n + 1], axis=-1, keepdims=True)
        res = jnp.where(lane == n, tot * (1.0 / MOBA_BLOCK), res)
    o_ref[0] = res


def _kmean_paged(page_table, cache_kt, layer):
    bs, n_pages = page_table.shape
    page = cache_kt.shape[3]
    nb = n_pages // 2
    return pl.pallas_call(
        functools.partial(_kmean_paged_kernel, layer=layer, n_pages=n_pages),
        grid_spec=pltpu.PrefetchScalarGridSpec(
            num_scalar_prefetch=1,
            grid=(bs,),
            in_specs=[pl.BlockSpec(memory_space=pl.ANY)],
            out_specs=pl.BlockSpec((1, A_WIDTH, nb), lambda b, pt: (b, 0, 0)),
            scratch_shapes=[pltpu.VMEM((n_pages, A_WIDTH, page), F32), pltpu.SemaphoreType.DMA((n_pages,))],
        ),
        out_shape=jax.ShapeDtypeStruct((bs, A_WIDTH, nb), F32),
        compiler_params=_cparams(("arbitrary",)),
        name="kmean_paged",
    )(page_table, cache_kt)


def _select_kernel(q_ref, km_ref, o_ref, *, bs, nb):
    head = lax.broadcasted_iota(jnp.int32, (A_HEADS, A_WIDTH), 0)
    chan = lax.broadcasted_iota(jnp.int32, (A_HEADS, A_WIDTH), 1)
    own = (chan >= head * HEAD_DIM) & (chan < (head + 1) * HEAD_DIM)
    bid = lax.broadcasted_iota(jnp.int32, (A_HEADS, nb), 1)
    bidf = bid.astype(F32)
    lane = lax.broadcasted_iota(jnp.int32, (A_HEADS, 128), 1)

    def body(b, _):
        qm = jnp.where(own, q_ref[pl.ds(b, 1), :], 0.0)
        gate = _mxu(qm, km_ref[b], True)
        rank = jnp.zeros((A_HEADS, nb), F32)
        for m in range(nb):
            gm = gate[:, m:m + 1]
            beats = (gm > gate) | ((gm == gate) & (bid > m))
            rank = rank + jnp.where(beats, 1.0, 0.0)
        res = jnp.zeros((A_HEADS, 128), F32)
        for r in range(MOBA_TOPK):
            idx = jnp.sum(jnp.where(rank == float(r), bidf, 0.0), axis=-1, keepdims=True)
            res = jnp.where(lane == r, idx, res)
        o_ref[b] = res.astype(jnp.int32)
        return 0

    lax.fori_loop(0, bs, body, 0)


def _select(qf, kmean_t):
    bs, _, nb = kmean_t.shape
    return pl.pallas_call(
        functools.partial(_select_kernel, bs=bs, nb=nb),
        out_shape=jax.ShapeDtypeStruct((bs, A_HEADS, 128), jnp.int32),
        compiler_params=pltpu.CompilerParams(vmem_limit_bytes=VMEM_LIMIT),
        name="moba_select",
    )(qf, kmean_t)


def _decode_kernel(pt_ref, sel_ref, q_ref, kn_ref, vn_ref, ck_ref, cv_ref, o_ref, kbuf, vbuf, sem, *, layer, pg):
    b = pl.program_id(0)
    copies = []
    for h in range(A_HEADS):
        for r in range(MOBA_TOPK):
            blk = sel_ref[b, h * MOBA_TOPK + r]
            for e in range(2):
                page = pt_ref[b, 2 * blk + e]
                for src, dst, s in ((ck_ref, kbuf, 0), (cv_ref, vbuf, 1)):
                    c = pltpu.make_async_copy(src.at[layer, page, pl.ds(h * HEAD_DIM, HEAD_DIM), :],
                                              dst.at[h, :, pl.ds((2 * r + e) * pg, pg)], sem.at[s])
                    c.start()
                    copies.append(c)
    for c in copies:
        c.wait()
    q = q_ref[0]
    kn = kn_ref[0]
    vn = vn_ref[0]
    for h in range(A_HEADS):
        s = _mxu(q, kbuf[h], True)[h:h + 1]
        s_self = jnp.sum(q[h:h + 1] * kn[h:h + 1], axis=-1, keepdims=True)
        m = jnp.maximum(jnp.max(s, axis=-1, keepdims=True), s_self)
        p = jnp.exp2(s - m)
        p_self = jnp.exp2(s_self - m)
        l = jnp.sum(p, axis=-1, keepdims=True) + p_self
        pv = _mxu_nt(jnp.broadcast_to(p, (8, p.shape[1])), vbuf[h])[0:1]
        o_ref[0, h:h + 1, :] = (pv + p_self * vn[h:h + 1]) / l


def _decode_attn(page_table, sel, qf, k_new, v_new, cache_kt, cache_vt, layer):
    bs = qf.shape[0]
    pg = cache_kt.shape[3]
    heads = lambda t: t.reshape(bs, A_HEADS, HEAD_DIM)
    row = pl.BlockSpec((1, A_HEADS, HEAD_DIM), lambda b, pt, sl: (b, 0, 0))
    gathered = pltpu.VMEM((A_HEADS, HEAD_DIM, 2 * MOBA_TOPK * pg), F32)
    out = pl.pallas_call(
        functools.partial(_decode_kernel, layer=layer, pg=pg),
        grid_spec=pltpu.PrefetchScalarGridSpec(
            num_scalar_prefetch=2,
            grid=(bs,),
            in_specs=[row, row, row, pl.BlockSpec(memory_space=pl.ANY), pl.BlockSpec(memory_space=pl.ANY)],
            out_specs=row,
            scratch_shapes=[gathered, gathered, pltpu.SemaphoreType.DMA((2,))],
        ),
        out_shape=jax.ShapeDtypeStruct((bs, A_HEADS, HEAD_DIM), F32),
        compiler_params=_cparams(("arbitrary",)),
        name="moba_decode",
    )(page_table, sel, heads(qf), heads(k_new), heads(v_new), cache_kt, cache_vt)
    return out.reshape(bs, A_WIDTH)


def _rwkv_kernel(rkv_ref, lora_ref, prkv_ref, plora_ref, s0_ref, mu_rkv_ref, mu_lora_ref, w0_ref, wdec_ref,
                 a0_ref, wa_ref, wg_ref, kk_ref, ka_ref, rk_ref, lnw_ref, lnb_ref, g_ref, tri_ref,
                 ob_ref, s_ref, c_rkv, c_lora, s_an, s_r, s_b, s_k, s_v, s_ld, s_y, *, T, C):
    i = pl.program_id(1)

    @pl.when(i == 0)
    def _():
        c_rkv[...] = prkv_ref[0]
        c_lora[...] = plora_ref[0]
        s_ref[0] = s0_ref[0]

    x = rkv_ref[0]
    xl = lora_ref[0]
    rowi = lax.broadcasted_iota(jnp.int32, (T, 1), 0)

    def shifted(cur, prev):
        return jnp.where(rowi == 0, prev, pltpu.roll(cur, 1, 0))

    xs = shifted(x, c_rkv[...])
    xls = shifted(xl, c_lora[...])
    c_rkv[...] = x[T - 1:T]
    c_lora[...] = xl[T - 1:T]
    pm = x + (xs - x) * mu_rkv_ref[...]
    pml = xl + (xls - xl) * mu_lora_ref[...]
    r = pm[:, 0:R_WIDTH]
    k = pm[:, R_WIDTH:2 * R_WIDTH]
    v = pm[:, 2 * R_WIDTH:3 * R_WIDTH]
    t01 = pml[:, 0:128]
    dec_arg = w0_ref[...] + _dot(jnp.tanh(t01).astype(BF16), wdec_ref[...])
    sp = jnp.maximum(-dec_arg, 0.0) + jnp.log(1.0 + jnp.exp(-jnp.abs(dec_arg)))
    ld = -jnp.exp(-sp - 0.5)
    a = jax.nn.sigmoid(a0_ref[...] + _dot(t01.astype(BF16), wa_ref[...]))
    g = _dot(jax.nn.sigmoid(pml[:, 128:LORA_PAD]).astype(BF16), wg_ref[...])
    gmat = g_ref[...]
    kk0 = k * kk_ref[...]
    kk = kk0 * lax.rsqrt(jnp.maximum(_dot_x2(kk0 * kk0, gmat), 1e-24))
    kmod = k * (1.0 + (a - 1.0) * ka_ref[...])
    bb = kk * a
    s_an[...] = -kk
    s_r[...] = r
    s_b[...] = bb
    s_k[...] = kmod
    s_v[...] = v
    s_ld[...] = ld

    tri = tri_ref[...]
    lane = lax.broadcasted_iota(jnp.int32, (C, 128), 1)
    in_h0 = lane < HEAD_DIM
    ri = lax.broadcasted_iota(jnp.int32, (2 * C, 2 * C), 0)
    ci = lax.broadcasted_iota(jnp.int32, (2 * C, 2 * C), 1)
    same = (ri >= C) == (ci >= C)
    strict = same & (ri > ci)
    incl = same & (ri >= ci)
    eye = jnp.where(ri == ci, 1.0, 0.0)

    def stack_masked(t):
        return jnp.concatenate([jnp.where(in_h0, t, 0.0), jnp.where(in_h0, 0.0, t)], axis=0)

    def chunk(c, _):
        rs = pl.ds(pl.multiple_of(c * C, C), C)
        ld_c = s_ld[rs, :]
        ld_hi, ld_mid, ld_lo = _split3(ld_c)
        cum = _dot(tri, ld_hi) + _dot(tri, ld_mid) + _dot(tri, ld_lo)
        cum_last = cum[C - 1:C]
        e_in = jnp.exp(cum)
        e_ex = jnp.exp(cum - ld_c)
        e_neg = jnp.exp(-cum)
        e_end = jnp.exp(cum_last - cum)
        g_end = jnp.exp(cum_last)
        at = s_an[rs, :] * e_ex
        rt = s_r[rs, :] * e_in
        bt = s_b[rs, :] * e_neg
        kt = s_k[rs, :] * e_neg
        bg = s_b[rs, :] * e_end
        kg = s_k[rs, :] * e_end
        vv = s_v[rs, :]
        pairs = range(4)
        lss = [slice(p * 128, (p + 1) * 128) for p in pairs]
        bf = lambda xs: [x.astype(BF16) for x in xs]
        ar_b = bf([jnp.concatenate([stack_masked(at[:, ls]), stack_masked(rt[:, ls])], axis=0) for ls in lss])
        bk_b = bf([jnp.concatenate([bt[:, ls], bt[:, ls], kt[:, ls], kt[:, ls]], axis=0) for ls in lss])
        vv_b = bf([stack_masked(vv[:, ls]) for ls in lss])
        bkg_b = bf([jnp.concatenate([stack_masked(bg[:, ls]), stack_masked(kg[:, ls])], axis=0) for ls in lss])
        big = [_dot_nt(a, b) for a, b in zip(ar_b, bk_b)]
        a_ab = [jnp.where(strict, m[0:2 * C, 0:2 * C], 0.0) for m in big]
        a_ak = bf([jnp.where(strict, m[0:2 * C, 2 * C:4 * C], 0.0) for m in big])
        a_rbk = bf([jnp.concatenate([jnp.where(incl, m[2 * C:4 * C, 0:2 * C], 0.0),
                                     jnp.where(incl, m[2 * C:4 * C, 2 * C:4 * C], 0.0)], axis=1) for m in big])
        inv = [eye + a for a in a_ab]
        pw = a_ab
        for _ in range(5):
            pw_b = bf(pw)
            pw = [_dot(x, x) for x in pw_b]
            inv = [iv + _dot(x.astype(BF16), iv.astype(BF16)) for x, iv in zip(pw, inv)]
        st = [s_ref[0, p] for p in pairs]
        from_state = [_dot_nt(a, s.astype(BF16)) for a, s in zip(ar_b, st)]
        rhs = [fs[0:2 * C] + _dot(a, v) for fs, a, v in zip(from_state, a_ak, vv_b)]
        u_b = bf([_dot(iv.astype(BF16), r.astype(BF16)) for iv, r in zip(inv, rhs)])
        uv_b = [jnp.concatenate([u, v], axis=0) for u, v in zip(u_b, vv_b)]
        y_s = [fs[2 * C:4 * C] + _dot(a, uv) for fs, a, uv in zip(from_state, a_rbk, uv_b)]
        for p in pairs:
            s_y[rs, lss[p]] = y_s[p][0:C] + y_s[p][C:2 * C]
            s_ref[0, p] = st[p] * g_end[:, lss[p]] + _dot_tn(uv_b[p], bkg_b[p])
        return 0

    lax.fori_loop(0, T // C, chunk, 0, unroll=True)

    y = s_y[...]
    mean = _dot_x2(y, gmat) * (1.0 / HEAD_DIM)
    d = y - mean
    var = _dot_x2(d * d, gmat) * (1.0 / HEAD_DIM)
    yn = d * lax.rsqrt(var + GN_EPS) * lnw_ref[...] + lnb_ref[...]
    bonus = _dot_x2(r * kmod * rk_ref[...], gmat) * v
    ob_ref[0] = ((yn + bonus) * g).astype(BF16)


def _rwkv(rkv, lora, prev_rkv, prev_lora, s0_bd, wts):
    bsz, seq, _ = rkv.shape
    C = RWKV_CHUNK
    T = min(256, seq)
    head_of = jnp.arange(R_WIDTH, dtype=jnp.int32) // HEAD_DIM
    gmat = (head_of[:, None] == head_of[None, :]).astype(BF16)
    tri = (jnp.arange(C)[:, None] >= jnp.arange(C)[None, :]).astype(BF16)
    consts = list(wts) + [gmat, tri]
    seq_spec = lambda w: pl.BlockSpec((1, T, w), lambda b, i: (b, i, 0))
    per_b = lambda a: pl.BlockSpec((1,) + a.shape[1:], lambda b, i: (b,) + (0,) * (a.ndim - 1))
    full = lambda a: pl.BlockSpec(a.shape, lambda b, i: (0,) * a.ndim)
    big = lambda: pltpu.VMEM((T, R_WIDTH), F32)
    return pl.pallas_call(
        functools.partial(_rwkv_kernel, T=T, C=C),
        grid=(bsz, seq // T),
        in_specs=[seq_spec(RKV_W), seq_spec(LORA_PAD), per_b(prev_rkv), per_b(prev_lora), per_b(s0_bd)]
        + [full(a) for a in consts],
        out_specs=[seq_spec(R_WIDTH), per_b(s0_bd)],
        out_shape=[jax.ShapeDtypeStruct((bsz, seq, R_WIDTH), BF16), jax.ShapeDtypeStruct(s0_bd.shape, F32)],
        scratch_shapes=[pltpu.VMEM((1, RKV_W), F32), pltpu.VMEM((1, LORA_PAD), F32)] + [big() for _ in range(7)],
        compiler_params=_cparams(("parallel", "arbitrary")),
        name="rwkv",
    )(rkv, lora, prev_rkv, prev_lora, s0_bd, *consts)


def _rwkv_step_kernel(rkv_ref, lora_ref, prkv_ref, plora_ref, s0_ref, mu_rkv_ref, mu_lora_ref, w0_ref, wdec_ref,
                      a0_ref, wa_ref, wg_ref, kk_ref, ka_ref, rk_ref, lnw_ref, lnb_ref,
                      ob_ref, s_ref, p_r, p_w, p_k, p_v, p_kk, p_a, p_g):
    b = pl.program_id(0)

    @pl.when(b == 0)
    def _():
        x = rkv_ref[...]
        xl = lora_ref[...]
        pm = x + (prkv_ref[...] - x) * mu_rkv_ref[...]
        pml = xl + (plora_ref[...] - xl) * mu_lora_ref[...]
        k = pm[:, R_WIDTH:2 * R_WIDTH]
        t01 = pml[:, 0:128]
        dec_arg = w0_ref[...] + _mxu(jnp.tanh(t01), wdec_ref[...], True)
        sp = jnp.maximum(-dec_arg, 0.0) + jnp.log(1.0 + jnp.exp(-jnp.abs(dec_arg)))
        a = jax.nn.sigmoid(a0_ref[...] + _mxu(t01, wa_ref[...], True))
        p_r[...] = pm[:, 0:R_WIDTH]
        p_w[...] = jnp.exp(-jnp.exp(-sp - 0.5))
        p_k[...] = k * (1.0 + (a - 1.0) * ka_ref[...])
        p_v[...] = pm[:, 2 * R_WIDTH:3 * R_WIDTH]
        p_kk[...] = k * kk_ref[...]
        p_a[...] = a
        p_g[...] = _mxu(jax.nn.sigmoid(pml[:, 128:LORA_PAD]), wg_ref[...], True)
        ob_ref[...] = jnp.zeros(ob_ref.shape, F32)

    rows = pl.ds(pl.multiple_of((b // 8) * 8, 8), 8)
    mine = lax.broadcasted_iota(jnp.int32, (8, 128), 0) == b % 8
    ri = lax.broadcasted_iota(jnp.int32, (128, 128), 0)
    ci = lax.broadcasted_iota(jnp.int32, (128, 128), 1)
    same_head = (ri >= HEAD_DIM) == (ci >= HEAD_DIM)
    eye = jnp.where(ri == ci, 1.0, 0.0)
    top = lax.broadcasted_iota(jnp.int32, (128, 1), 0) < HEAD_DIM
    left = lax.broadcasted_iota(jnp.int32, (1, 128), 1) < HEAD_DIM

    def per_head_rows(t):
        s0 = jnp.sum(jnp.where(left, t, 0.0), axis=-1, keepdims=True)
        s1 = jnp.sum(jnp.where(left, 0.0, t), axis=-1, keepdims=True)
        return jnp.where(left, s0, s1)

    def per_head_cols(t):
        s0 = jnp.sum(jnp.where(top, t, 0.0), axis=0, keepdims=True)
        s1 = jnp.sum(jnp.where(top, 0.0, t), axis=0, keepdims=True)
        return jnp.where(top, s0, s1)

    for p in range(4):
        ls = slice(p * 128, (p + 1) * 128)
        r, w, k, v, kk0, a, g = (jnp.sum(jnp.where(mine, t[rows, ls], 0.0), axis=0, keepdims=True)
                                 for t in (p_r, p_w, p_k, p_v, p_kk, p_a, p_g))
        kk = kk0 * lax.rsqrt(jnp.maximum(per_head_rows(kk0 * kk0), 1e-24))
        st = s0_ref[0, p]
        sa = jnp.sum(st * kk, axis=-1, keepdims=True)
        v_col = jnp.sum(eye * v, axis=-1, keepdims=True)
        st = st * w + jnp.where(same_head, v_col * k - sa * (kk * a), 0.0)
        s_ref[0, p] = st
        y = jnp.sum(st * r, axis=-1, keepdims=True)
        d = y - per_head_cols(y) * (1.0 / HEAD_DIM)
        yn = d * lax.rsqrt(per_head_cols(d * d) * (1.0 / HEAD_DIM) + GN_EPS)
        yn_row = jnp.sum(eye * yn, axis=0, keepdims=True)
        bonus = per_head_rows(r * k * rk_ref[:, ls]) * v
        out = (yn_row * lnw_ref[:, ls] + lnb_ref[:, ls] + bonus) * g
        ob_ref[rows, ls] = jnp.where(mine, out, ob_ref[rows, ls])


def _rwkv_step(rkv, lora, prev_rkv, prev_lora, s0_bd, wts):
    bs = rkv.shape[0]
    full = lambda a: pl.BlockSpec(a.shape, lambda b: (0,) * a.ndim)
    state = pl.BlockSpec((1,) + s0_bd.shape[1:], lambda b: (b, 0, 0, 0))
    args = [rkv, lora, prev_rkv, prev_lora]
    return pl.pallas_call(
        _rwkv_step_kernel,
        grid=(bs,),
        in_specs=[full(a) for a in args] + [state] + [full(a) for a in wts],
        out_specs=[pl.BlockSpec((bs, R_WIDTH), lambda b: (0, 0)), state],
        out_shape=[jax.ShapeDtypeStruct((bs, R_WIDTH), F32), jax.ShapeDtypeStruct(s0_bd.shape, F32)],
        scratch_shapes=[pltpu.VMEM((bs, R_WIDTH), F32) for _ in range(7)],
        compiler_params=_cparams(("arbitrary",)),
        name="rwkv_step",
    )(*args, s0_bd, *wts)


def _state_to_bd(s):
    b = s.shape[0]
    s = s.reshape(b, 4, 2, HEAD_DIM, HEAD_DIM)
    z = jnp.zeros_like(s[:, :, 0])
    top = jnp.concatenate([s[:, :, 0], z], axis=-1)
    bot = jnp.concatenate([z, s[:, :, 1]], axis=-1)
    return jnp.concatenate([top, bot], axis=-2)


def _state_from_bd(s):
    b = s.shape[0]
    h0 = s[:, :, :HEAD_DIM, :HEAD_DIM]
    h1 = s[:, :, HEAD_DIM:, HEAD_DIM:]
    return jnp.stack([h0, h1], axis=2).reshape(b, 8, HEAD_DIM, HEAD_DIM)


def _merge_kernel(x_ref, gain_ref, wg_ref, oa_ref, ob_ref, wa_ref, wb_ref, wo_ref, o_ref, *, precise):
    x = x_ref[...]
    d = x.shape[1]
    gates = jax.nn.sigmoid(_mxu(_rms(x, gain_ref[...]), wg_ref[...], precise))
    ya = _mxu(oa_ref[...], wa_ref[...], precise)
    yb = _mxu(ob_ref[...], wb_ref[...], precise)
    merged = gates[:, :d] * ya + gates[:, d:] * yb
    o_ref[...] = x + _mxu(merged, wo_ref[...], precise)


def _merge(x, gain, wg, oa, ob, wa, wb, wo, precise=False):
    n, d = x.shape
    tm = min(512, n)
    row = lambda w: pl.BlockSpec((tm, w), lambda i: (i, 0))
    full = lambda a: pl.BlockSpec(a.shape, lambda i: (0, 0))
    return pl.pallas_call(
        functools.partial(_merge_kernel, precise=precise),
        grid=(n // tm,),
        in_specs=[row(d), full(gain), full(wg), row(A_WIDTH), row(R_WIDTH), full(wa), full(wb), full(wo)],
        out_specs=row(d),
        out_shape=jax.ShapeDtypeStruct((n, d), F32),
        compiler_params=_cparams(("parallel",)),
        name="merge",
    )(x, gain, wg, oa, ob, wa, wb, wo)


def _ffn_kernel(x_ref, gain_ref, w1_ref, w3_ref, w2_ref, o_ref, h_scr, acc, *, precise):
    f = pl.program_id(1)

    @pl.when(f == 0)
    def _():
        h_scr[...] = _rms(x_ref[...], gain_ref[...]).astype(h_scr.dtype)
        acc[...] = x_ref[...]

    h = h_scr[...]
    t = jax.nn.silu(_mxu(h, w1_ref[...], precise)) * _mxu(h, w3_ref[...], precise)
    acc[...] += _mxu(t, w2_ref[...], precise)

    @pl.when(f == pl.num_programs(1) - 1)
    def _():
        o_ref[...] = acc[...]


def _ffn(x, gain, w1, w3, w2, precise=False):
    n, d = x.shape
    dff = w1.shape[1]
    tm = min(512, n)
    tf = dff // 2
    return pl.pallas_call(
        functools.partial(_ffn_kernel, precise=precise),
        grid=(n // tm, dff // tf),
        in_specs=[pl.BlockSpec((tm, d), lambda i, f: (i, 0)), pl.BlockSpec((1, d), lambda i, f: (0, 0)),
                  pl.BlockSpec((d, tf), lambda i, f: (0, f)), pl.BlockSpec((d, tf), lambda i, f: (0, f)),
                  pl.BlockSpec((tf, d), lambda i, f: (f, 0))],
        out_specs=pl.BlockSpec((tm, d), lambda i, f: (i, 0)),
        out_shape=jax.ShapeDtypeStruct((n, d), F32),
        scratch_shapes=[pltpu.VMEM((tm, d), F32 if precise else BF16), pltpu.VMEM((tm, d), F32)],
        compiler_params=_cparams(("parallel", "arbitrary")),
        name="ffn",
    )(x, gain, w1, w3, w2)


def _router_kernel(x_ref, gain_ref, wr_ref, o_ref, *, n_exp):
    logits = _mxu(_rms(x_ref[...], gain_ref[...]), wr_ref[...], True)
    lane = lax.broadcasted_iota(jnp.int32, logits.shape, 1)
    lanef = lane.astype(F32)
    logits = jnp.where(lane < n_exp, logits, -jnp.inf)
    m1 = jnp.max(logits, axis=-1, keepdims=True)
    e1 = jnp.min(jnp.where(logits == m1, lanef, 1e9), axis=-1, keepdims=True)
    rest = jnp.where(lanef == e1, -jnp.inf, logits)
    m2 = jnp.max(rest, axis=-1, keepdims=True)
    e2 = jnp.min(jnp.where(rest == m2, lanef, 1e9), axis=-1, keepdims=True)
    z = jnp.exp(m2 - m1)
    g1 = 1.0 / (1.0 + z)
    g2 = z / (1.0 + z)
    res = jnp.where(lane == 0, e1, jnp.where(lane == 1, e2, jnp.where(lane == 2, g1, jnp.where(lane == 3, g2, 0.0))))
    o_ref[...] = res[:, 0:8]


def _router(x, gain, w_router):
    n, d = x.shape
    n_exp = w_router.shape[1]
    tm = min(512, n)
    wp = jnp.zeros((d, 128), F32).at[:, :n_exp].set(w_router)
    return pl.pallas_call(
        functools.partial(_router_kernel, n_exp=n_exp),
        grid=(n // tm,),
        in_specs=[pl.BlockSpec((tm, d), lambda i: (i, 0)), pl.BlockSpec((1, d), lambda i: (0, 0)),
                  pl.BlockSpec((d, 128), lambda i: (0, 0))],
        out_specs=pl.BlockSpec((tm, 8), lambda i: (i, 0)),
        out_shape=jax.ShapeDtypeStruct((n, 8), F32),
        compiler_params=_cparams(("parallel",)),
        name="router",
    )(x, gain, wp)


def _experts_kernel(be_ref, xb_ref, gain_ref, rg_ref, w1_ref, w3_ref, w2_ref, o_ref, *, precise):
    h = _rms(xb_ref[...], gain_ref[...])
    h = h if precise else h.astype(BF16)
    t = jax.nn.silu(_mxu(h, w1_ref[0], precise)) * _mxu(h, w3_ref[0], precise)
    o_ref[...] = _mxu(t, w2_ref[0], precise) * rg_ref[...]


def _experts(blk_e, xb, gain, row_gate, w1, w3, w2, rb, precise):
    r, d = xb.shape
    dffe = w1.shape[2]
    return pl.pallas_call(
        functools.partial(_experts_kernel, precise=precise),
        grid_spec=pltpu.PrefetchScalarGridSpec(
            num_scalar_prefetch=1,
            grid=(r // rb,),
            in_specs=[pl.BlockSpec((rb, d), lambda i, be: (i, 0)), pl.BlockSpec((1, d), lambda i, be: (0, 0)),
                      pl.BlockSpec((rb, 1), lambda i, be: (i, 0)),
                      pl.BlockSpec((1, d, dffe), lambda i, be: (be[i], 0, 0)),
                      pl.BlockSpec((1, d, dffe), lambda i, be: (be[i], 0, 0)),
                      pl.BlockSpec((1, dffe, d), lambda i, be: (be[i], 0, 0))],
            out_specs=pl.BlockSpec((rb, d), lambda i, be: (i, 0)),
        ),
        out_shape=jax.ShapeDtypeStruct((r, d), F32),
        compiler_params=_cparams(("arbitrary",)),
        name="experts",
    )(blk_e, xb, gain, row_gate, w1, w3, w2)


def _moe(x, gain, w_router, w1, w3, w2, rb, precise=False):
    n, d = x.shape
    n_exp = w_router.shape[1]
    route = _router(x, gain, w_router)
    top_e = route[:, 0:MOE_TOPK].astype(jnp.int32)
    gate = route[:, MOE_TOPK:2 * MOE_TOPK]
    na = n * MOE_TOPK
    e_flat = top_e.reshape(na)
    g_flat = gate.reshape(na)
    onehot = (e_flat[:, None] == jnp.arange(n_exp, dtype=jnp.int32)[None, :]).astype(jnp.int32)
    within = jnp.cumsum(onehot, axis=0) - onehot
    counts = jnp.sum(onehot, axis=0)
    padded = (counts + rb - 1) // rb * rb
    pend = jnp.cumsum(padded)
    pstart = pend - padded
    dest = pstart[e_flat] + jnp.sum(within * onehot, axis=1)
    r = (na + n_exp * (rb - 1) + rb - 1) // rb * rb
    tok = jnp.arange(na, dtype=jnp.int32) // MOE_TOPK
    row_tok = jnp.zeros((r,), jnp.int32).at[dest].set(tok)
    row_gate = jnp.zeros((r,), F32).at[dest].set(g_flat)
    blk_start = jnp.arange(r // rb, dtype=jnp.int32) * rb
    blk_e = jnp.minimum(jnp.searchsorted(pend, blk_start, side="right"), n_exp - 1).astype(jnp.int32)
    xb = x[row_tok]
    yb = _experts(blk_e, xb, gain, row_gate[:, None], w1, w3, w2, rb, precise)
    pos = dest.reshape(n, MOE_TOPK)
    return x + yb[pos[:, 0]] + yb[pos[:, 1]]


def _norm_kernel(x_ref, gain_ref, o_ref):
    o_ref[...] = _rms(x_ref[...], gain_ref[...])


def _final_norm(x, gain):
    n, d = x.shape
    tm = min(1024, n)
    return pl.pallas_call(
        _norm_kernel,
        grid=(n // tm,),
        in_specs=[pl.BlockSpec((tm, d), lambda i: (i, 0)), pl.BlockSpec((1, d), lambda i: (0, 0))],
        out_specs=pl.BlockSpec((tm, d), lambda i: (i, 0)),
        out_shape=jax.ShapeDtypeStruct((n, d), F32),
        compiler_params=_cparams(("parallel",)),
        name="final_norm",
    )(x, gain)


MIXER_WEIGHTS = ("norm_attn", "w_in", "w_up_a", "mu_shift", "w0", "w_dec2", "a0", "w_a2", "w_g2", "k_k", "k_a", "r_k",
                 "ln_x_w", "ln_x_b", "w_up_b", "w_out")


def _layer_weights(l, dt, norm_attn, w_in, w_up_a, mu_shift, w0, w_dec2, a0, w_a2, w_g2, k_k, k_a, r_k, ln_x_w, ln_x_b,
                   w_up_b, w_out):
    d = w_in.shape[1]
    wi = w_in[l]
    c0 = 3 * A_WIDTH
    c1 = c0 + RKV_W
    n_lora = LORA_DECAY + LORA_ICLR + LORA_GATE
    c2 = c1 + n_lora
    wqkv = wi[:, :c0].astype(dt)
    wrkv = wi[:, c0:c1].astype(dt)
    wlora = jnp.zeros((d, LORA_PAD), dt).at[:, :n_lora].set(wi[:, c1:c2].astype(dt))
    wgate = wi[:, c2:].astype(dt)
    mu = mu_shift[l]
    mu_rkv = mu[:RKV_W][None]
    mu_lora = jnp.zeros((1, LORA_PAD), F32).at[0, :n_lora].set(mu[RKV_W:])
    wdec = jnp.zeros((128, R_WIDTH), dt).at[:LORA_DECAY].set(w_dec2[l].astype(dt))
    wa = jnp.zeros((128, R_WIDTH), dt).at[LORA_DECAY:LORA_DECAY + LORA_ICLR].set(w_a2[l].astype(dt))
    wg = jnp.zeros((LORA_PAD - 128, R_WIDTH), dt).at[:LORA_GATE].set(w_g2[l].astype(dt))
    rw = (mu_rkv, mu_lora, w0[l][None], wdec, a0[l][None], wa, wg, k_k[l][None], k_a[l][None],
          r_k[l].reshape(1, R_WIDTH), ln_x_w[l][None], ln_x_b[l][None])
    return dict(gain=norm_attn[l][None], wqkv=wqkv, wrkv=wrkv, wlora=wlora, wgate=wgate, rw=rw,
                wa=w_up_a[l].astype(dt), wb=w_up_b[l].astype(dt), wo=w_out[l].astype(dt), n_lora=n_lora)


def _mixer_prompt(x, bsz, seq, w):
    qt, kb, va, k, v, rkv, lora = _proj(x, w["gain"], w["wqkv"], w["wrkv"], w["wlora"], seq=seq)
    oa = _moba_prompt(qt, kb, va, _kmean(k), bsz, seq)
    zeros = lambda width: jnp.zeros((bsz, 1, width), F32)
    ob, s_bd = _rwkv(rkv.reshape(bsz, seq, RKV_W), lora.reshape(bsz, seq, LORA_PAD), zeros(RKV_W), zeros(LORA_PAD),
                     jnp.zeros((bsz, 4, 128, 128), F32), w["rw"])
    x = _merge(x, w["gain"], w["wgate"], oa, ob.reshape(bsz * seq, R_WIDTH), w["wa"], w["wb"], w["wo"])
    last = lambda t: t.reshape(bsz, seq, -1)[:, -1]
    p_last = jnp.concatenate([last(rkv), last(lora)[:, :w["n_lora"]]], axis=-1)
    return x, k, v, _state_from_bd(s_bd), p_last


def _mixer_sample(x, page_table, cache_kt, cache_vt, layer, s0, p_prev, w):
    bs = x.shape[0]
    q, k, v, rkv, lora = _proj(x, w["gain"], w["wqkv"], w["wrkv"], w["wlora"], precise=True)
    kmean_t = _kmean_paged(page_table, cache_kt, layer)
    sel = _select(q, kmean_t)[:, :, :MOBA_TOPK].reshape(bs, A_HEADS * MOBA_TOPK)
    oa = _decode_attn(page_table, sel, q, k, v, cache_kt, cache_vt, layer)
    prev_lora = jnp.zeros((bs, LORA_PAD), F32).at[:, :w["n_lora"]].set(p_prev[:, RKV_W:])
    ob, s_bd = _rwkv_step(rkv, lora, p_prev[:, :RKV_W], prev_lora, _state_to_bd(s0), w["rw"])
    x = _merge(x, w["gain"], w["wgate"], oa, ob, w["wa"], w["wb"], w["wo"], precise=True)
    p_last = jnp.concatenate([rkv, lora[:, :w["n_lora"]]], axis=-1)
    return x, k, v, _state_from_bd(s_bd), p_last


def _prompt_group(x_prompt, wts):
    bp, seq, d = x_prompt.shape
    x = x_prompt.reshape(bp * seq, d)
    outs = [[] for _ in range(4)]
    for l in range(wts["w_in"].shape[0]):
        w = _layer_weights(l, BF16, *(wts[n] for n in MIXER_WEIGHTS))
        x, k, v, s, sh = _mixer_prompt(x, bp, seq, w)
        for o, t in zip(outs, (k.reshape(bp, seq, A_HEADS, HEAD_DIM), v.reshape(bp, seq, A_HEADS, HEAD_DIM), s, sh)):
            o.append(t)
        gain = wts["norm_ffn"][l][None]
        i = l // 2
        bf = lambda t: t[i].astype(BF16)
        if l % 2 == 0:
            x = _ffn(x, gain, bf(wts["ffn_w1"]), bf(wts["ffn_w3"]), bf(wts["ffn_w2"]))
        else:
            x = _moe(x, gain, wts["router"][i], bf(wts["moe_w1"]), bf(wts["moe_w3"]), bf(wts["moe_w2"]), 256)
    y = _final_norm(x, wts["norm_final"][None]).reshape(bp, seq, d)
    return (y,) + tuple(jnp.stack(o) for o in outs)


def _sample_group(x_sample, cache_k, cache_v, state_wkv, state_shift, page_table, wts):
    bs, dec_seq, d = x_sample.shape
    depth, n_pool, page = cache_k.shape[:3]
    assert dec_seq == 1 and 2 * page == MOBA_BLOCK
    cache_kt = cache_k.transpose(0, 1, 3, 4, 2).reshape(depth, n_pool, A_WIDTH, page)
    cache_vt = cache_v.transpose(0, 1, 3, 4, 2).reshape(depth, n_pool, A_WIDTH, page)
    x = x_sample.reshape(bs, d)
    outs = [[] for _ in range(4)]
    for l in range(depth):
        w = _layer_weights(l, F32, *(wts[n] for n in MIXER_WEIGHTS))
        x, k, v, s, sh = _mixer_sample(x, page_table, cache_kt, cache_vt, l, state_wkv[l], state_shift[l], w)
        for o, t in zip(outs, (k.reshape(bs, 1, A_HEADS, HEAD_DIM), v.reshape(bs, 1, A_HEADS, HEAD_DIM), s, sh)):
            o.append(t)
        gain = wts["norm_ffn"][l][None]
        i = l // 2
        if l % 2 == 0:
            x = _ffn(x, gain, wts["ffn_w1"][i], wts["ffn_w3"][i], wts["ffn_w2"][i], precise=True)
        else:
            x = _moe(x, gain, wts["router"][i], wts["moe_w1"][i], wts["moe_w3"][i], wts["moe_w2"][i], 32, precise=True)
    y = _final_norm(x, wts["norm_final"][None]).reshape(bs, 1, d)
    return (y,) + tuple(jnp.stack(o) for o in outs)


def kernel(x_prompt, x_sample, cache_k, cache_v, state_wkv, state_shift, page_table, norm_attn, w_in, w_up_a, mu_shift, w0, w_dec2, a0, w_a2, w_g2, k_k, k_a, r_k, ln_x_w, ln_x_b, w_up_b, w_out, norm_ffn, ffn_w1, ffn_w3, ffn_w2, router, moe_w1, moe_w3, moe_w2, norm_final):
    assert x_prompt.shape[1] % MOBA_BLOCK == 0
    wts = dict(norm_attn=norm_attn, w_in=w_in, w_up_a=w_up_a, mu_shift=mu_shift, w0=w0, w_dec2=w_dec2, a0=a0, w_a2=w_a2,
               w_g2=w_g2, k_k=k_k, k_a=k_a, r_k=r_k, ln_x_w=ln_x_w, ln_x_b=ln_x_b, w_up_b=w_up_b, w_out=w_out,
               norm_ffn=norm_ffn, ffn_w1=ffn_w1, ffn_w3=ffn_w3, ffn_w2=ffn_w2, router=router, moe_w1=moe_w1,
               moe_w3=moe_w3, moe_w2=moe_w2, norm_final=norm_final)
    y_p, k_p, v_p, s_p, sh_p = _prompt_group(x_prompt, wts)
    y_s, k_s, v_s, s_s, sh_s = _sample_group(x_sample, cache_k, cache_v, state_wkv, state_shift, page_table, wts)
    return (y_p, y_s, k_p, v_p, s_p, sh_p, k_s, v_s, s_s, sh_s)
```

```python
import functools

import jax
import jax.numpy as jnp
from jax import lax
from jax.experimental import pallas as pl
from jax.experimental.pallas import tpu as pltpu

F32 = jnp.float32
BF16 = jnp.bfloat16

A_HEADS = 8
HEAD_DIM = 64
A_WIDTH = 512
R_WIDTH = 512
MOBA_BLOCK = 256
MOBA_TOPK = 3
LORA_DECAY = 64
LORA_ICLR = 64
LORA_GATE = 160
LORA_PAD = 384
RKV_W = 3 * R_WIDTH
GN_EPS = 64e-5
RMS_EPS = 1e-6
MOE_TOPK = 2
RWKV_CHUNK = 64
NEG = -1e30
Q_SCALE = HEAD_DIM ** -0.5 * 1.4426950408889634
V_AUG = HEAD_DIM + 16
VMEM_LIMIT = 56 * 1024 * 1024


def _dot(a, b):
    return jnp.dot(a, b, preferred_element_type=F32)


def _dot_nt(a, b):
    return lax.dot_general(a, b, (((1,), (1,)), ((), ())), preferred_element_type=F32)


def _dot_tn(a, b):
    return lax.dot_general(a, b, (((0,), (0,)), ((), ())), preferred_element_type=F32)


def _mxu(a, b, precise):
    if precise:
        return jnp.dot(a.astype(F32), b.astype(F32), precision=lax.Precision.HIGHEST, preferred_element_type=F32)
    return jnp.dot(a.astype(BF16), b.astype(BF16), preferred_element_type=F32)


def _mxu_nt(a, b):
    return lax.dot_general(a, b, (((1,), (1,)), ((), ())), precision=lax.Precision.HIGHEST,
                           preferred_element_type=F32)


def _split2(x):
    hi = x.astype(BF16)
    lo = (x - hi.astype(F32)).astype(BF16)
    return hi, lo


def _split3(x):
    hi = x.astype(BF16)
    r1 = x - hi.astype(F32)
    mid = r1.astype(BF16)
    lo = (r1 - mid.astype(F32)).astype(BF16)
    return hi, mid, lo


def _dot_x2(x, m):
    hi, lo = _split2(x)
    return _dot(hi, m) + _dot(lo, m)


def _rms(x, gain):
    return x * lax.rsqrt(jnp.mean(x * x, axis=-1, keepdims=True) + RMS_EPS) * gain


def _cparams(sem):
    return pltpu.CompilerParams(dimension_semantics=sem, vmem_limit_bytes=VMEM_LIMIT)


def _proj_kernel(x_ref, gain_ref, wqkv_ref, wrkv_ref, wlora_ref, q_ref, *outs, precise):
    k_ref, v_ref, rkv_ref, lora_ref = outs[-4:]
    h = _rms(x_ref[...], gain_ref[...])
    h = h if precise else h.astype(BF16)
    q = _mxu(h, wqkv_ref[:, 0:A_WIDTH], precise) * Q_SCALE
    k = _mxu(h, wqkv_ref[:, A_WIDTH:2 * A_WIDTH], precise)
    v = _mxu(h, wqkv_ref[:, 2 * A_WIDTH:3 * A_WIDTH], precise)
    k_ref[...] = k
    v_ref[...] = v
    rkv_ref[...] = _mxu(h, wrkv_ref[...], precise)
    lora_ref[...] = _mxu(h, wlora_ref[...], precise)
    if precise:
        q_ref[...] = q
        return
    kb_ref, vb_ref = outs[:2]
    q_ref[0] = q.T.astype(BF16)
    kb_ref[...] = k.astype(BF16)
    ones = jnp.ones((V_AUG - HEAD_DIM, MOBA_BLOCK), F32)
    for u in range(v.shape[0] // MOBA_BLOCK):
        vt = v[u * MOBA_BLOCK:(u + 1) * MOBA_BLOCK].T
        parts = []
        for hd in range(A_HEADS):
            parts += [vt[hd * HEAD_DIM:(hd + 1) * HEAD_DIM], ones]
        vb_ref[0, u] = jnp.concatenate(parts, axis=0).astype(BF16)


def _proj(x, gain, wqkv, wrkv, wlora, seq=None, precise=False):
    n, d = x.shape
    tm = min(512, n)
    row = lambda w: pl.BlockSpec((tm, w), lambda i: (i, 0))
    full = lambda a: pl.BlockSpec(a.shape, lambda i: (0, 0))
    if precise:
        attn_specs = [row(A_WIDTH)]
        attn_shapes = [jax.ShapeDtypeStruct((n, A_WIDTH), F32)]
    else:
        tps = seq // tm
        nbt = tm // MOBA_BLOCK
        attn_specs = [pl.BlockSpec((1, A_WIDTH, tm), lambda i: (i // tps, 0, i % tps)), row(A_WIDTH),
                      pl.BlockSpec((1, nbt, A_HEADS * V_AUG, MOBA_BLOCK), lambda i: (i // tps, i % tps, 0, 0))]
        attn_shapes = [jax.ShapeDtypeStruct((n // seq, A_WIDTH, seq), BF16), jax.ShapeDtypeStruct((n, A_WIDTH), BF16),
                       jax.ShapeDtypeStruct((n // seq, seq // MOBA_BLOCK, A_HEADS * V_AUG, MOBA_BLOCK), BF16)]
    return pl.pallas_call(
        functools.partial(_proj_kernel, precise=precise),
        grid=(n // tm,),
        in_specs=[row(d), full(gain), full(wqkv), full(wrkv), full(wlora)],
        out_specs=attn_specs + [row(A_WIDTH)] * 2 + [row(RKV_W), row(LORA_PAD)],
        out_shape=attn_shapes + [jax.ShapeDtypeStruct((n, A_WIDTH), F32)] * 2
        + [jax.ShapeDtypeStruct((n, RKV_W), F32), jax.ShapeDtypeStruct((n, LORA_PAD), F32)],
        compiler_params=_cparams(("parallel",)),
        name="proj",
    )(x, gain, wqkv, wrkv, wlora)


def _kmean_kernel(k_ref, o_ref):
    g = o_ref.shape[0]
    x = k_ref[...].reshape(g, MOBA_BLOCK, A_WIDTH)
    o_ref[...] = jnp.sum(x, axis=1) * (1.0 / MOBA_BLOCK)


def _kmean(k):
    n = k.shape[0]
    nblk = n // MOBA_BLOCK
    g = min(8, nblk)
    return pl.pallas_call(
        _kmean_kernel,
        grid=(nblk // g,),
        in_specs=[pl.BlockSpec((g * MOBA_BLOCK, A_WIDTH), lambda i: (i, 0))],
        out_specs=pl.BlockSpec((g, A_WIDTH), lambda i: (i, 0)),
        out_shape=jax.ShapeDtypeStruct((nblk, A_WIDTH), F32),
        compiler_params=_cparams(("parallel",)),
        name="kmean",
    )(k)


def _moba_kernel(qt_ref, k_ref, va_ref, km_ref, o_ref, bias_ref, s_scr, *, nb, unroll):
    i = pl.program_id(2)
    blk = MOBA_BLOCK
    qt = qt_ref[0]
    row = lax.broadcasted_iota(jnp.int32, qt.shape, 0)
    km_hi, km_lo = _split2(km_ref[0])
    bid = lax.broadcasted_iota(jnp.int32, (nb, blk), 0)
    qms = []
    for hh in range(2):
        qm = jnp.where((row >= hh * HEAD_DIM) & (row < (hh + 1) * HEAD_DIM), qt, jnp.zeros_like(qt))
        qms.append(qm)
        gate = _dot(km_hi, qm) + _dot(km_lo, qm)
        gate = jnp.where(bid < i, gate, -jnp.inf)
        rank = jnp.zeros((nb, blk), F32)
        for m in range(nb):
            gm = gate[m:m + 1, :]
            beats = (gm > gate) | ((gm == gate) & (bid > m))
            rank = rank + jnp.where(beats, 1.0, 0.0)
        sel = (bid < i) & (rank < float(MOBA_TOPK))
        bias_ref[hh] = jnp.where(sel, 0.0, NEG)

    kio = lax.broadcasted_iota(jnp.int32, (blk, blk), 0)
    qio = lax.broadcasted_iota(jnp.int32, (blk, blk), 1)
    causal = kio <= qio
    kd = k_ref[0, i]
    ms = []
    for hh in range(2):
        s = jnp.where(causal, _dot(kd, qms[hh]), NEG)
        s_scr[hh, nb] = s
        ms.append(jnp.max(s, axis=0, keepdims=True))
    n_trips = (i + unroll - 1) // unroll

    def scores(jj, ms):
        ms = list(ms)
        for u in range(unroll):
            j = unroll * jj + u
            kj = k_ref[0, j]
            for hh in range(2):
                s = _dot(kj, qms[hh]) + bias_ref[hh, pl.ds(j, 1), :]
                s_scr[hh, j] = s
                ms[hh] = jnp.maximum(ms[hh], jnp.max(s, axis=0, keepdims=True))
        return tuple(ms)

    ms = lax.fori_loop(0, n_trips, scores, tuple(ms))

    vd = va_ref[0, i]
    accs = []
    for hh in range(2):
        p = jnp.exp2(s_scr[hh, nb] - ms[hh]).astype(BF16)
        accs.append(_dot(vd[hh * V_AUG:(hh + 1) * V_AUG, :], p))

    def weighted(jj, accs):
        accs = list(accs)
        for u in range(unroll):
            j = unroll * jj + u
            vj = va_ref[0, j]
            for hh in range(2):
                p = jnp.exp2(s_scr[hh, j] - ms[hh]).astype(BF16)
                accs[hh] = accs[hh] + _dot(vj[hh * V_AUG:(hh + 1) * V_AUG, :], p)
        return tuple(accs)

    accs = lax.fori_loop(0, n_trips, weighted, tuple(accs))
    o = jnp.concatenate([a[0:HEAD_DIM] / a[HEAD_DIM:HEAD_DIM + 1] for a in accs], axis=0)
    o_ref[0] = o.T.astype(BF16)


def _moba_prompt(qt, kb, va, kmean, bsz, seq):
    nb = seq // MOBA_BLOCK
    k4 = kb.reshape(bsz, nb, MOBA_BLOCK, A_WIDTH)
    km = kmean.reshape(bsz, nb, A_WIDTH)
    out = pl.pallas_call(
        functools.partial(_moba_kernel, nb=nb, unroll=4 if nb % 4 == 0 else 2),
        grid=(bsz, A_HEADS // 2, nb),
        in_specs=[
            pl.BlockSpec((1, 128, MOBA_BLOCK), lambda b, h, i: (b, h, i)),
            pl.BlockSpec((1, nb, MOBA_BLOCK, 128), lambda b, h, i: (b, 0, 0, h)),
            pl.BlockSpec((1, nb, 2 * V_AUG, MOBA_BLOCK), lambda b, h, i: (b, 0, h, 0)),
            pl.BlockSpec((1, nb, 128), lambda b, h, i: (b, 0, h)),
        ],
        out_specs=pl.BlockSpec((1, MOBA_BLOCK, 128), lambda b, h, i: (b, i, h)),
        out_shape=jax.ShapeDtypeStruct((bsz, seq, A_WIDTH), BF16),
        scratch_shapes=[pltpu.VMEM((2, nb, MOBA_BLOCK), F32), pltpu.VMEM((2, nb + 1, MOBA_BLOCK, MOBA_BLOCK), F32)],
        compiler_params=_cparams(("parallel", "parallel", "arbitrary")),
        name="moba_prompt",
    )(qt, k4, va, km)
    return out.reshape(bsz * seq, A_WIDTH)


def _kmean_paged_kernel(pt_ref, ck_ref, o_ref, buf, sem, *, layer, n_pages):
    b = pl.program_id(0)
    slot = b % 2

    def fetch(seq, to_slot):
        for p in range(n_pages):
            pltpu.make_async_copy(ck_ref.at[layer, pt_ref[seq, p]], buf.at[to_slot, p], sem.at[to_slot, p]).start()

    @pl.when(b == 0)
    def _():
        fetch(0, 0)

    @pl.when(b + 1 < pl.num_programs(0))
    def _():
        fetch(b + 1, 1 - slot)

    nb = n_pages // 2
    lane = lax.broadcasted_iota(jnp.int32, (A_WIDTH, nb), 1)
    res = jnp.zeros((A_WIDTH, nb), F32)
    for n in range(nb):
        for e in range(2):
            pltpu.make_async_copy(ck_ref.at[layer, 0], buf.at[slot, 2 * n + e], sem.at[slot, 2 * n + e]).wait()
        tot = jnp.sum(buf[slot, 2 * n] + buf[slot, 2 * n + 1], axis=-1, keepdims=True)
        res = jnp.where(lane == n, tot * (1.0 / MOBA_BLOCK), res)
    o_ref[0] = res


def _kmean_paged(page_table, cache_kt, layer):
    bs, n_pages = page_table.shape
    page = cache_kt.shape[3]
    nb = n_pages // 2
    return pl.pallas_call(
        functools.partial(_kmean_paged_kernel, layer=layer, n_pages=n_pages),
        grid_spec=pltpu.PrefetchScalarGridSpec(
            num_scalar_prefetch=1,
            grid=(bs,),
            in_specs=[pl.BlockSpec(memory_space=pl.ANY)],
            out_specs=pl.BlockSpec((1, A_WIDTH, nb), lambda b, pt: (b, 0, 0)),
            scratch_shapes=[pltpu.VMEM((2, n_pages, A_WIDTH, page), F32), pltpu.SemaphoreType.DMA((2, n_pages))],
        ),
        out_shape=jax.ShapeDtypeStruct((bs, A_WIDTH, nb), F32),
        compiler_params=_cparams(("arbitrary",)),
        name="kmean_paged",
    )(page_table, cache_kt)


def _select_kernel(q_ref, km_ref, o_ref, *, bs, nb):
    head = lax.broadcasted_iota(jnp.int32, (A_HEADS, A_WIDTH), 0)
    chan = lax.broadcasted_iota(jnp.int32, (A_HEADS, A_WIDTH), 1)
    own = (chan >= head * HEAD_DIM) & (chan < (head + 1) * HEAD_DIM)
    bid = lax.broadcasted_iota(jnp.int32, (A_HEADS, nb), 1)
    bidf = bid.astype(F32)
    lane = lax.broadcasted_iota(jnp.int32, (A_HEADS, 128), 1)

    def body(b, _):
        qm = jnp.where(own, q_ref[pl.ds(b, 1), :], 0.0)
        gate = _mxu(qm, km_ref[b], True)
        rank = jnp.zeros((A_HEADS, nb), F32)
        for m in range(nb):
            gm = gate[:, m:m + 1]
            beats = (gm > gate) | ((gm == gate) & (bid > m))
            rank = rank + jnp.where(beats, 1.0, 0.0)
        res = jnp.zeros((A_HEADS, 128), F32)
        for r in range(MOBA_TOPK):
            idx = jnp.sum(jnp.where(rank == float(r), bidf, 0.0), axis=-1, keepdims=True)
            res = jnp.where(lane == r, idx, res)
        o_ref[b] = res.astype(jnp.int32)
        return 0

    lax.fori_loop(0, bs, body, 0)


def _select(qf, kmean_t):
    bs, _, nb = kmean_t.shape
    return pl.pallas_call(
        functools.partial(_select_kernel, bs=bs, nb=nb),
        out_shape=jax.ShapeDtypeStruct((bs, A_HEADS, 128), jnp.int32),
        compiler_params=pltpu.CompilerParams(vmem_limit_bytes=VMEM_LIMIT),
        name="moba_select",
    )(qf, kmean_t)


def _decode_kernel(pt_ref, sel_ref, q_ref, kn_ref, vn_ref, ck_ref, cv_ref, o_ref, kbuf, vbuf, sem, *, layer, pg):
    t = pl.program_id(0)
    n_seq = pl.num_programs(0) - 1

    def copies(seq, slot):
        out = []
        for h in range(A_HEADS):
            for r in range(MOBA_TOPK):
                blk = sel_ref[seq, h * MOBA_TOPK + r]
                for e in range(2):
                    page = pt_ref[seq, 2 * blk + e]
                    for src, dst, s in ((ck_ref, kbuf, 0), (cv_ref, vbuf, 1)):
                        out.append(pltpu.make_async_copy(src.at[layer, page, pl.ds(h * HEAD_DIM, HEAD_DIM), :],
                                                         dst.at[slot, h, :, pl.ds((2 * r + e) * pg, pg)],
                                                         sem.at[slot, s]))
        return out

    @pl.when(t < n_seq)
    def _():
        for c in copies(t, t % 2):
            c.start()

    @pl.when(t > 0)
    def _():
        slot = (t - 1) % 2
        for c in copies(t - 1, slot):
            c.wait()
        q = q_ref[0]
        kn = kn_ref[0]
        vn = vn_ref[0]
        for h in range(A_HEADS):
            s = _mxu(q, kbuf[slot, h], True)[h:h + 1]
            s_self = jnp.sum(q[h:h + 1] * kn[h:h + 1], axis=-1, keepdims=True)
            m = jnp.maximum(jnp.max(s, axis=-1, keepdims=True), s_self)
            p = jnp.exp2(s - m)
            p_self = jnp.exp2(s_self - m)
            l = jnp.sum(p, axis=-1, keepdims=True) + p_self
            pv = _mxu_nt(jnp.broadcast_to(p, (8, p.shape[1])), vbuf[slot, h])[0:1]
            o_ref[0, h:h + 1, :] = (pv + p_self * vn[h:h + 1]) / l


def _decode_attn(page_table, sel, qf, k_new, v_new, cache_kt, cache_vt, layer):
    bs = qf.shape[0]
    pg = cache_kt.shape[3]
    heads = lambda t: t.reshape(bs, A_HEADS, HEAD_DIM)
    row = pl.BlockSpec((1, A_HEADS, HEAD_DIM), lambda t, pt, sl: (jnp.maximum(t - 1, 0), 0, 0))
    gathered = pltpu.VMEM((2, A_HEADS, HEAD_DIM, 2 * MOBA_TOPK * pg), F32)
    out = pl.pallas_call(
        functools.partial(_decode_kernel, layer=layer, pg=pg),
        grid_spec=pltpu.PrefetchScalarGridSpec(
            num_scalar_prefetch=2,
            grid=(bs + 1,),
            in_specs=[row, row, row, pl.BlockSpec(memory_space=pl.ANY), pl.BlockSpec(memory_space=pl.ANY)],
            out_specs=row,
            scratch_shapes=[gathered, gathered, pltpu.SemaphoreType.DMA((2, 2))],
        ),
        out_shape=jax.ShapeDtypeStruct((bs, A_HEADS, HEAD_DIM), F32),
        compiler_params=_cparams(("arbitrary",)),
        name="moba_decode",
    )(page_table, sel, heads(qf), heads(k_new), heads(v_new), cache_kt, cache_vt)
    return out.reshape(bs, A_WIDTH)


def _rwkv_kernel(rkv_ref, lora_ref, prkv_ref, plora_ref, s0_ref, mu_rkv_ref, mu_lora_ref, w0_ref, wdec_ref,
                 a0_ref, wa_ref, wg_ref, kk_ref, ka_ref, rk_ref, lnw_ref, lnb_ref, g_ref, tri_ref,
                 ob_ref, s_ref, c_rkv, c_lora, s_an, s_r, s_b, s_k, s_v, s_ld, s_y, *, T, C):
    i = pl.program_id(1)

    @pl.when(i == 0)
    def _():
        c_rkv[...] = prkv_ref[0]
        c_lora[...] = plora_ref[0]
        s_ref[0] = s0_ref[0]

    x = rkv_ref[0]
    xl = lora_ref[0]
    rowi = lax.broadcasted_iota(jnp.int32, (T, 1), 0)

    def shifted(cur, prev):
        return jnp.where(rowi == 0, prev, pltpu.roll(cur, 1, 0))

    xs = shifted(x, c_rkv[...])
    xls = shifted(xl, c_lora[...])
    c_rkv[...] = x[T - 1:T]
    c_lora[...] = xl[T - 1:T]
    pm = x + (xs - x) * mu_rkv_ref[...]
    pml = xl + (xls - xl) * mu_lora_ref[...]
    r = pm[:, 0:R_WIDTH]
    k = pm[:, R_WIDTH:2 * R_WIDTH]
    v = pm[:, 2 * R_WIDTH:3 * R_WIDTH]
    t01 = pml[:, 0:128]
    dec_arg = w0_ref[...] + _dot(jnp.tanh(t01).astype(BF16), wdec_ref[...])
    sp = jnp.maximum(-dec_arg, 0.0) + jnp.log(1.0 + jnp.exp(-jnp.abs(dec_arg)))
    ld = -jnp.exp(-sp - 0.5)
    a = jax.nn.sigmoid(a0_ref[...] + _dot(t01.astype(BF16), wa_ref[...]))
    g = _dot(jax.nn.sigmoid(pml[:, 128:LORA_PAD]).astype(BF16), wg_ref[...])
    gmat = g_ref[...]
    kk0 = k * kk_ref[...]
    kk = kk0 * lax.rsqrt(jnp.maximum(_dot_x2(kk0 * kk0, gmat), 1e-24))
    kmod = k * (1.0 + (a - 1.0) * ka_ref[...])
    bb = kk * a
    s_an[...] = -kk
    s_r[...] = r
    s_b[...] = bb
    s_k[...] = kmod
    s_v[...] = v
    s_ld[...] = ld

    tri = tri_ref[...]
    lane = lax.broadcasted_iota(jnp.int32, (C, 128), 1)
    in_h0 = lane < HEAD_DIM
    ri = lax.broadcasted_iota(jnp.int32, (2 * C, 2 * C), 0)
    ci = lax.broadcasted_iota(jnp.int32, (2 * C, 2 * C), 1)
    same = (ri >= C) == (ci >= C)
    strict = same & (ri > ci)
    incl = same & (ri >= ci)
    eye = jnp.where(ri == ci, 1.0, 0.0)

    def stack_masked(t):
        return jnp.concatenate([jnp.where(in_h0, t, 0.0), jnp.where(in_h0, 0.0, t)], axis=0)

    def chunk(c, _):
        rs = pl.ds(pl.multiple_of(c * C, C), C)
        ld_c = s_ld[rs, :]
        ld_hi, ld_mid, ld_lo = _split3(ld_c)
        cum = _dot(tri, ld_hi) + _dot(tri, ld_mid) + _dot(tri, ld_lo)
        cum_last = cum[C - 1:C]
        e_in = jnp.exp(cum)
        e_ex = jnp.exp(cum - ld_c)
        e_neg = jnp.exp(-cum)
        e_end = jnp.exp(cum_last - cum)
        g_end = jnp.exp(cum_last)
        at = s_an[rs, :] * e_ex
        rt = s_r[rs, :] * e_in
        bt = s_b[rs, :] * e_neg
        kt = s_k[rs, :] * e_neg
        bg = s_b[rs, :] * e_end
        kg = s_k[rs, :] * e_end
        vv = s_v[rs, :]
        pairs = range(4)
        lss = [slice(p * 128, (p + 1) * 128) for p in pairs]
        bf = lambda xs: [x.astype(BF16) for x in xs]
        ar_b = bf([jnp.concatenate([stack_masked(at[:, ls]), stack_masked(rt[:, ls])], axis=0) for ls in lss])
        bk_b = bf([jnp.concatenate([bt[:, ls], bt[:, ls], kt[:, ls], kt[:, ls]], axis=0) for ls in lss])
        vv_b = bf([stack_masked(vv[:, ls]) for ls in lss])
        bkg_b = bf([jnp.concatenate([stack_masked(bg[:, ls]), stack_masked(kg[:, ls])], axis=0) for ls in lss])
        big = [_dot_nt(a, b) for a, b in zip(ar_b, bk_b)]
        a_ab = [jnp.where(strict, m[0:2 * C, 0:2 * C], 0.0) for m in big]
        a_ak = bf([jnp.where(strict, m[0:2 * C, 2 * C:4 * C], 0.0) for m in big])
        a_rbk = bf([jnp.concatenate([jnp.where(incl, m[2 * C:4 * C, 0:2 * C], 0.0),
                                     jnp.where(incl, m[2 * C:4 * C, 2 * C:4 * C], 0.0)], axis=1) for m in big])
        inv = [eye + a for a in a_ab]
        pw = a_ab
        for _ in range(5):
            pw_b = bf(pw)
            pw = [_dot(x, x) for x in pw_b]
            inv = [iv + _dot(x.astype(BF16), iv.astype(BF16)) for x, iv in zip(pw, inv)]
        st = [s_ref[0, p] for p in pairs]
        from_state = [_dot_nt(a, s.astype(BF16)) for a, s in zip(ar_b, st)]
        rhs = [fs[0:2 * C] + _dot(a, v) for fs, a, v in zip(from_state, a_ak, vv_b)]
        u_b = bf([_dot(iv.astype(BF16), r.astype(BF16)) for iv, r in zip(inv, rhs)])
        uv_b = [jnp.concatenate([u, v], axis=0) for u, v in zip(u_b, vv_b)]
        y_s = [fs[2 * C:4 * C] + _dot(a, uv) for fs, a, uv in zip(from_state, a_rbk, uv_b)]
        for p in pairs:
            s_y[rs, lss[p]] = y_s[p][0:C] + y_s[p][C:2 * C]
            s_ref[0, p] = st[p] * g_end[:, lss[p]] + _dot_tn(uv_b[p], bkg_b[p])
        return 0

    lax.fori_loop(0, T // C, chunk, 0, unroll=True)

    y = s_y[...]
    mean = _dot_x2(y, gmat) * (1.0 / HEAD_DIM)
    d = y - mean
    var = _dot_x2(d * d, gmat) * (1.0 / HEAD_DIM)
    yn = d * lax.rsqrt(var + GN_EPS) * lnw_ref[...] + lnb_ref[...]
    bonus = _dot_x2(r * kmod * rk_ref[...], gmat) * v
    ob_ref[0] = ((yn + bonus) * g).astype(BF16)


def _rwkv(rkv, lora, prev_rkv, prev_lora, s0_bd, wts):
    bsz, seq, _ = rkv.shape
    C = RWKV_CHUNK
    T = min(256, seq)
    head_of = jnp.arange(R_WIDTH, dtype=jnp.int32) // HEAD_DIM
    gmat = (head_of[:, None] == head_of[None, :]).astype(BF16)
    tri = (jnp.arange(C)[:, None] >= jnp.arange(C)[None, :]).astype(BF16)
    consts = list(wts) + [gmat, tri]
    seq_spec = lambda w: pl.BlockSpec((1, T, w), lambda b, i: (b, i, 0))
    per_b = lambda a: pl.BlockSpec((1,) + a.shape[1:], lambda b, i: (b,) + (0,) * (a.ndim - 1))
    full = lambda a: pl.BlockSpec(a.shape, lambda b, i: (0,) * a.ndim)
    big = lambda: pltpu.VMEM((T, R_WIDTH), F32)
    return pl.pallas_call(
        functools.partial(_rwkv_kernel, T=T, C=C),
        grid=(bsz, seq // T),
        in_specs=[seq_spec(RKV_W), seq_spec(LORA_PAD), per_b(prev_rkv), per_b(prev_lora), per_b(s0_bd)]
        + [full(a) for a in consts],
        out_specs=[seq_spec(R_WIDTH), per_b(s0_bd)],
        out_shape=[jax.ShapeDtypeStruct((bsz, seq, R_WIDTH), BF16), jax.ShapeDtypeStruct(s0_bd.shape, F32)],
        scratch_shapes=[pltpu.VMEM((1, RKV_W), F32), pltpu.VMEM((1, LORA_PAD), F32)] + [big() for _ in range(7)],
        compiler_params=_cparams(("parallel", "arbitrary")),
        name="rwkv",
    )(rkv, lora, prev_rkv, prev_lora, s0_bd, *consts)


def _rwkv_step_kernel(rkv_ref, lora_ref, prkv_ref, plora_ref, s0_ref, mu_rkv_ref, mu_lora_ref, w0_ref, wdec_ref,
                      a0_ref, wa_ref, wg_ref, kk_ref, ka_ref, rk_ref, lnw_ref, lnb_ref,
                      ob_ref, s_ref, p_r, p_w, p_k, p_v, p_kk, p_a, p_g):
    b = pl.program_id(0)

    @pl.when(b == 0)
    def _():
        x = rkv_ref[...]
        xl = lora_ref[...]
        pm = x + (prkv_ref[...] - x) * mu_rkv_ref[...]
        pml = xl + (plora_ref[...] - xl) * mu_lora_ref[...]
        k = pm[:, R_WIDTH:2 * R_WIDTH]
        t01 = pml[:, 0:128]
        dec_arg = w0_ref[...] + _mxu(jnp.tanh(t01), wdec_ref[...], True)
        sp = jnp.maximum(-dec_arg, 0.0) + jnp.log(1.0 + jnp.exp(-jnp.abs(dec_arg)))
        a = jax.nn.sigmoid(a0_ref[...] + _mxu(t01, wa_ref[...], True))
        p_r[...] = pm[:, 0:R_WIDTH]
        p_w[...] = jnp.exp(-jnp.exp(-sp - 0.5))
        p_k[...] = k * (1.0 + (a - 1.0) * ka_ref[...])
        p_v[...] = pm[:, 2 * R_WIDTH:3 * R_WIDTH]
        p_kk[...] = k * kk_ref[...]
        p_a[...] = a
        p_g[...] = _mxu(jax.nn.sigmoid(pml[:, 128:LORA_PAD]), wg_ref[...], True)
        ob_ref[...] = jnp.zeros(ob_ref.shape, F32)

    rows = pl.ds(pl.multiple_of((b // 8) * 8, 8), 8)
    mine = lax.broadcasted_iota(jnp.int32, (8, 128), 0) == b % 8
    ri = lax.broadcasted_iota(jnp.int32, (128, 128), 0)
    ci = lax.broadcasted_iota(jnp.int32, (128, 128), 1)
    same_head = (ri >= HEAD_DIM) == (ci >= HEAD_DIM)
    eye = jnp.where(ri == ci, 1.0, 0.0)
    top = lax.broadcasted_iota(jnp.int32, (128, 1), 0) < HEAD_DIM
    left = lax.broadcasted_iota(jnp.int32, (1, 128), 1) < HEAD_DIM

    def per_head_rows(t):
        s0 = jnp.sum(jnp.where(left, t, 0.0), axis=-1, keepdims=True)
        s1 = jnp.sum(jnp.where(left, 0.0, t), axis=-1, keepdims=True)
        return jnp.where(left, s0, s1)

    def per_head_cols(t):
        s0 = jnp.sum(jnp.where(top, t, 0.0), axis=0, keepdims=True)
        s1 = jnp.sum(jnp.where(top, 0.0, t), axis=0, keepdims=True)
        return jnp.where(top, s0, s1)

    for p in range(4):
        ls = slice(p * 128, (p + 1) * 128)
        r, w, k, v, kk0, a, g = (jnp.sum(jnp.where(mine, t[rows, ls], 0.0), axis=0, keepdims=True)
                                 for t in (p_r, p_w, p_k, p_v, p_kk, p_a, p_g))
        kk = kk0 * lax.rsqrt(jnp.maximum(per_head_rows(kk0 * kk0), 1e-24))
        st = s0_ref[0, p]
        sa = jnp.sum(st * kk, axis=-1, keepdims=True)
        v_col = jnp.sum(eye * v, axis=-1, keepdims=True)
        st = st * w + jnp.where(same_head, v_col * k - sa * (kk * a), 0.0)
        s_ref[0, p] = st
        y = jnp.sum(st * r, axis=-1, keepdims=True)
        d = y - per_head_cols(y) * (1.0 / HEAD_DIM)
        yn = d * lax.rsqrt(per_head_cols(d * d) * (1.0 / HEAD_DIM) + GN_EPS)
        yn_row = jnp.sum(eye * yn, axis=0, keepdims=True)
        bonus = per_head_rows(r * k * rk_ref[:, ls]) * v
        out = (yn_row * lnw_ref[:, ls] + lnb_ref[:, ls] + bonus) * g
        ob_ref[rows, ls] = jnp.where(mine, out, ob_ref[rows, ls])


def _rwkv_step(rkv, lora, prev_rkv, prev_lora, s0_bd, wts):
    bs = rkv.shape[0]
    full = lambda a: pl.BlockSpec(a.shape, lambda b: (0,) * a.ndim)
    state = pl.BlockSpec((1,) + s0_bd.shape[1:], lambda b: (b, 0, 0, 0))
    args = [rkv, lora, prev_rkv, prev_lora]
    return pl.pallas_call(
        _rwkv_step_kernel,
        grid=(bs,),
        in_specs=[full(a) for a in args] + [state] + [full(a) for a in wts],
        out_specs=[pl.BlockSpec((bs, R_WIDTH), lambda b: (0, 0)), state],
        out_shape=[jax.ShapeDtypeStruct((bs, R_WIDTH), F32), jax.ShapeDtypeStruct(s0_bd.shape, F32)],
        scratch_shapes=[pltpu.VMEM((bs, R_WIDTH), F32) for _ in range(7)],
        compiler_params=_cparams(("arbitrary",)),
        name="rwkv_step",
    )(*args, s0_bd, *wts)


def _state_to_bd(s):
    b = s.shape[0]
    s = s.reshape(b, 4, 2, HEAD_DIM, HEAD_DIM)
    z = jnp.zeros_like(s[:, :, 0])
    top = jnp.concatenate([s[:, :, 0], z], axis=-1)
    bot = jnp.concatenate([z, s[:, :, 1]], axis=-1)
    return jnp.concatenate([top, bot], axis=-2)


def _state_from_bd(s):
    b = s.shape[0]
    h0 = s[:, :, :HEAD_DIM, :HEAD_DIM]
    h1 = s[:, :, HEAD_DIM:, HEAD_DIM:]
    return jnp.stack([h0, h1], axis=2).reshape(b, 8, HEAD_DIM, HEAD_DIM)


def _merge_kernel(x_ref, gain_ref, wg_ref, oa_ref, ob_ref, wa_ref, wb_ref, wo_ref, o_ref, *, precise):
    x = x_ref[...]
    d = x.shape[1]
    gates = jax.nn.sigmoid(_mxu(_rms(x, gain_ref[...]), wg_ref[...], precise))
    ya = _mxu(oa_ref[...], wa_ref[...], precise)
    yb = _mxu(ob_ref[...], wb_ref[...], precise)
    merged = gates[:, :d] * ya + gates[:, d:] * yb
    o_ref[...] = x + _mxu(merged, wo_ref[...], precise)


def _merge(x, gain, wg, oa, ob, wa, wb, wo, precise=False):
    n, d = x.shape
    tm = min(512, n)
    row = lambda w: pl.BlockSpec((tm, w), lambda i: (i, 0))
    full = lambda a: pl.BlockSpec(a.shape, lambda i: (0, 0))
    return pl.pallas_call(
        functools.partial(_merge_kernel, precise=precise),
        grid=(n // tm,),
        in_specs=[row(d), full(gain), full(wg), row(A_WIDTH), row(R_WIDTH), full(wa), full(wb), full(wo)],
        out_specs=row(d),
        out_shape=jax.ShapeDtypeStruct((n, d), F32),
        compiler_params=_cparams(("parallel",)),
        name="merge",
    )(x, gain, wg, oa, ob, wa, wb, wo)


def _ffn_kernel(x_ref, gain_ref, w1_ref, w3_ref, w2_ref, o_ref, h_scr, acc, *, precise):
    f = pl.program_id(1)

    @pl.when(f == 0)
    def _():
        h_scr[...] = _rms(x_ref[...], gain_ref[...]).astype(h_scr.dtype)
        acc[...] = x_ref[...]

    h = h_scr[...]
    t = jax.nn.silu(_mxu(h, w1_ref[...], precise)) * _mxu(h, w3_ref[...], precise)
    acc[...] += _mxu(t, w2_ref[...], precise)

    @pl.when(f == pl.num_programs(1) - 1)
    def _():
        o_ref[...] = acc[...]


def _ffn(x, gain, w1, w3, w2, precise=False):
    n, d = x.shape
    dff = w1.shape[1]
    tm = min(512, n)
    tf = dff // 2
    return pl.pallas_call(
        functools.partial(_ffn_kernel, precise=precise),
        grid=(n // tm, dff // tf),
        in_specs=[pl.BlockSpec((tm, d), lambda i, f: (i, 0)), pl.BlockSpec((1, d), lambda i, f: (0, 0)),
                  pl.BlockSpec((d, tf), lambda i, f: (0, f)), pl.BlockSpec((d, tf), lambda i, f: (0, f)),
                  pl.BlockSpec((tf, d), lambda i, f: (f, 0))],
        out_specs=pl.BlockSpec((tm, d), lambda i, f: (i, 0)),
        out_shape=jax.ShapeDtypeStruct((n, d), F32),
        scratch_shapes=[pltpu.VMEM((tm, d), F32 if precise else BF16), pltpu.VMEM((tm, d), F32)],
        compiler_params=_cparams(("parallel", "arbitrary")),
        name="ffn",
    )(x, gain, w1, w3, w2)


def _router_kernel(x_ref, gain_ref, wr_ref, o_ref, *, n_exp):
    logits = _mxu(_rms(x_ref[...], gain_ref[...]), wr_ref[...], True)
    lane = lax.broadcasted_iota(jnp.int32, logits.shape, 1)
    lanef = lane.astype(F32)
    logits = jnp.where(lane < n_exp, logits, -jnp.inf)
    m1 = jnp.max(logits, axis=-1, keepdims=True)
    e1 = jnp.min(jnp.where(logits == m1, lanef, 1e9), axis=-1, keepdims=True)
    rest = jnp.where(lanef == e1, -jnp.inf, logits)
    m2 = jnp.max(rest, axis=-1, keepdims=True)
    e2 = jnp.min(jnp.where(rest == m2, lanef, 1e9), axis=-1, keepdims=True)
    z = jnp.exp(m2 - m1)
    g1 = 1.0 / (1.0 + z)
    g2 = z / (1.0 + z)
    res = jnp.where(lane == 0, e1, jnp.where(lane == 1, e2, jnp.where(lane == 2, g1, jnp.where(lane == 3, g2, 0.0))))
    o_ref[...] = res[:, 0:8]


def _router(x, gain, w_router):
    n, d = x.shape
    n_exp = w_router.shape[1]
    tm = min(512, n)
    wp = jnp.zeros((d, 128), F32).at[:, :n_exp].set(w_router)
    return pl.pallas_call(
        functools.partial(_router_kernel, n_exp=n_exp),
        grid=(n // tm,),
        in_specs=[pl.BlockSpec((tm, d), lambda i: (i, 0)), pl.BlockSpec((1, d), lambda i: (0, 0)),
                  pl.BlockSpec((d, 128), lambda i: (0, 0))],
        out_specs=pl.BlockSpec((tm, 8), lambda i: (i, 0)),
        out_shape=jax.ShapeDtypeStruct((n, 8), F32),
        compiler_params=_cparams(("parallel",)),
        name="router",
    )(x, gain, wp)


def _experts_kernel(be_ref, xb_ref, gain_ref, w1_ref, w3_ref, w2_ref, o_ref, *wb, precise):
    h = _rms(xb_ref[...], gain_ref[...])
    if precise:
        t = jax.nn.silu(_mxu(h, w1_ref[0], True)) * _mxu(h, w3_ref[0], True)
        o_ref[...] = _mxu(t, w2_ref[0], True)
        return
    i = pl.program_id(0)
    w1b, w3b, w2b = wb

    @pl.when((i == 0) | (be_ref[i] != be_ref[jnp.maximum(i - 1, 0)]))
    def _():
        w1b[...] = w1_ref[0].astype(BF16)
        w3b[...] = w3_ref[0].astype(BF16)
        w2b[...] = w2_ref[0].astype(BF16)

    h = h.astype(BF16)
    t = jax.nn.silu(_dot(h, w1b[...])) * _dot(h, w3b[...])
    o_ref[...] = _dot(t.astype(BF16), w2b[...])


def _experts(blk_e, xb, gain, w1, w3, w2, rb, precise):
    r, d = xb.shape
    dffe = w1.shape[2]
    wspec = lambda shape: pl.BlockSpec((1,) + shape, lambda i, be: (be[i], 0, 0), pipeline_mode=pl.Buffered(1))
    scratch = [] if precise else [pltpu.VMEM((d, dffe), BF16), pltpu.VMEM((d, dffe), BF16), pltpu.VMEM((dffe, d), BF16)]
    return pl.pallas_call(
        functools.partial(_experts_kernel, precise=precise),
        grid_spec=pltpu.PrefetchScalarGridSpec(
            num_scalar_prefetch=1,
            grid=(r // rb,),
            in_specs=[pl.BlockSpec((rb, d), lambda i, be: (i, 0)), pl.BlockSpec((1, d), lambda i, be: (0, 0)),
                      wspec((d, dffe)), wspec((d, dffe)), wspec((dffe, d))],
            out_specs=pl.BlockSpec((rb, d), lambda i, be: (i, 0)),
            scratch_shapes=scratch,
        ),
        out_shape=jax.ShapeDtypeStruct((r, d), F32),
        compiler_params=_cparams(("arbitrary",)),
        name="experts",
    )(blk_e, xb, gain, w1, w3, w2)


def _combine_kernel(x_ref, y0_ref, y1_ref, route_ref, *rest, final):
    g = route_ref[...]
    out = x_ref[...] + g[:, MOE_TOPK:MOE_TOPK + 1] * y0_ref[...] + g[:, MOE_TOPK + 1:MOE_TOPK + 2] * y1_ref[...]
    if final:
        gain_ref, o_ref = rest
        o_ref[...] = _rms(out, gain_ref[...])
    else:
        rest[0][...] = out


def _combine(x, y0, y1, route, final_gain):
    n, d = x.shape
    tm = min(512, n)
    row = pl.BlockSpec((tm, d), lambda i: (i, 0))
    extra = [] if final_gain is None else [final_gain]
    return pl.pallas_call(
        functools.partial(_combine_kernel, final=final_gain is not None),
        grid=(n // tm,),
        in_specs=[row, row, row, pl.BlockSpec((tm, 8), lambda i: (i, 0))] + [pl.BlockSpec((1, d), lambda i: (0, 0))] * len(extra),
        out_specs=row,
        out_shape=jax.ShapeDtypeStruct((n, d), F32),
        compiler_params=_cparams(("parallel",)),
        name="moe_combine",
    )(x, y0, y1, route, *extra)


def _moe(x, gain, w_router, w1, w3, w2, rb, precise=False, final_gain=None):
    n, d = x.shape
    n_exp = w_router.shape[1]
    route = _router(x, gain, w_router)
    top_e = route[:, 0:MOE_TOPK].astype(jnp.int32)
    na = n * MOE_TOPK
    e_flat = top_e.reshape(na)
    onehot = (e_flat[:, None] == jnp.arange(n_exp, dtype=jnp.int32)[None, :]).astype(jnp.int32)
    within = jnp.cumsum(onehot, axis=0) - onehot
    counts = jnp.sum(onehot, axis=0)
    padded = (counts + rb - 1) // rb * rb
    pend = jnp.cumsum(padded)
    pstart = pend - padded
    dest = pstart[e_flat] + jnp.sum(within * onehot, axis=1)
    r = (na + n_exp * (rb - 1) + rb - 1) // rb * rb
    tok = jnp.arange(na, dtype=jnp.int32) // MOE_TOPK
    row_tok = jnp.zeros((r,), jnp.int32).at[dest].set(tok)
    blk_start = jnp.arange(r // rb, dtype=jnp.int32) * rb
    blk_e = jnp.minimum(jnp.searchsorted(pend, blk_start, side="right"), n_exp - 1).astype(jnp.int32)
    yb = _experts(blk_e, x[row_tok], gain, w1, w3, w2, rb, precise)
    pos = dest.reshape(n, MOE_TOPK)
    return _combine(x, yb[pos[:, 0]], yb[pos[:, 1]], route, final_gain)


def _norm_kernel(x_ref, gain_ref, o_ref):
    o_ref[...] = _rms(x_ref[...], gain_ref[...])


def _final_norm(x, gain):
    n, d = x.shape
    tm = min(1024, n)
    return pl.pallas_call(
        _norm_kernel,
        grid=(n // tm,),
        in_specs=[pl.BlockSpec((tm, d), lambda i: (i, 0)), pl.BlockSpec((1, d), lambda i: (0, 0))],
        out_specs=pl.BlockSpec((tm, d), lambda i: (i, 0)),
        out_shape=jax.ShapeDtypeStruct((n, d), F32),
        compiler_params=_cparams(("parallel",)),
        name="final_norm",
    )(x, gain)


MIXER_WEIGHTS = ("norm_attn", "w_in", "w_up_a", "mu_shift", "w0", "w_dec2", "a0", "w_a2", "w_g2", "k_k", "k_a", "r_k",
                 "ln_x_w", "ln_x_b", "w_up_b", "w_out")


def _layer_weights(l, dt, norm_attn, w_in, w_up_a, mu_shift, w0, w_dec2, a0, w_a2, w_g2, k_k, k_a, r_k, ln_x_w, ln_x_b,
                   w_up_b, w_out):
    d = w_in.shape[1]
    wi = w_in[l]
    c0 = 3 * A_WIDTH
    c1 = c0 + RKV_W
    n_lora = LORA_DECAY + LORA_ICLR + LORA_GATE
    c2 = c1 + n_lora
    wqkv = wi[:, :c0].astype(dt)
    wrkv = wi[:, c0:c1].astype(dt)
    wlora = jnp.zeros((d, LORA_PAD), dt).at[:, :n_lora].set(wi[:, c1:c2].astype(dt))
    wgate = wi[:, c2:].astype(dt)
    mu = mu_shift[l]
    mu_rkv = mu[:RKV_W][None]
    mu_lora = jnp.zeros((1, LORA_PAD), F32).at[0, :n_lora].set(mu[RKV_W:])
    wdec = jnp.zeros((128, R_WIDTH), dt).at[:LORA_DECAY].set(w_dec2[l].astype(dt))
    wa = jnp.zeros((128, R_WIDTH), dt).at[LORA_DECAY:LORA_DECAY + LORA_ICLR].set(w_a2[l].astype(dt))
    wg = jnp.zeros((LORA_PAD - 128, R_WIDTH), dt).at[:LORA_GATE].set(w_g2[l].astype(dt))
    rw = (mu_rkv, mu_lora, w0[l][None], wdec, a0[l][None], wa, wg, k_k[l][None], k_a[l][None],
          r_k[l].reshape(1, R_WIDTH), ln_x_w[l][None], ln_x_b[l][None])
    return dict(gain=norm_attn[l][None], wqkv=wqkv, wrkv=wrkv, wlora=wlora, wgate=wgate, rw=rw,
                wa=w_up_a[l].astype(dt), wb=w_up_b[l].astype(dt), wo=w_out[l].astype(dt), n_lora=n_lora)


def _mixer_prompt(x, bsz, seq, w):
    qt, kb, va, k, v, rkv, lora = _proj(x, w["gain"], w["wqkv"], w["wrkv"], w["wlora"], seq=seq)
    oa = _moba_prompt(qt, kb, va, _kmean(k), bsz, seq)
    zeros = lambda width: jnp.zeros((bsz, 1, width), F32)
    ob, s_bd = _rwkv(rkv.reshape(bsz, seq, RKV_W), lora.reshape(bsz, seq, LORA_PAD), zeros(RKV_W), zeros(LORA_PAD),
                     jnp.zeros((bsz, 4, 128, 128), F32), w["rw"])
    x = _merge(x, w["gain"], w["wgate"], oa, ob.reshape(bsz * seq, R_WIDTH), w["wa"], w["wb"], w["wo"])
    last = lambda t: t.reshape(bsz, seq, -1)[:, -1]
    p_last = jnp.concatenate([last(rkv), last(lora)[:, :w["n_lora"]]], axis=-1)
    return x, k, v, _state_from_bd(s_bd), p_last


def _mixer_sample(x, page_table, cache_kt, cache_vt, layer, s0, p_prev, w):
    bs = x.shape[0]
    q, k, v, rkv, lora = _proj(x, w["gain"], w["wqkv"], w["wrkv"], w["wlora"], precise=True)
    kmean_t = _kmean_paged(page_table, cache_kt, layer)
    sel = _select(q, kmean_t)[:, :, :MOBA_TOPK].reshape(bs, A_HEADS * MOBA_TOPK)
    oa = _decode_attn(page_table, sel, q, k, v, cache_kt, cache_vt, layer)
    prev_lora = jnp.zeros((bs, LORA_PAD), F32).at[:, :w["n_lora"]].set(p_prev[:, RKV_W:])
    ob, s_bd = _rwkv_step(rkv, lora, p_prev[:, :RKV_W], prev_lora, _state_to_bd(s0), w["rw"])
    x = _merge(x, w["gain"], w["wgate"], oa, ob, w["wa"], w["wb"], w["wo"], precise=True)
    p_last = jnp.concatenate([rkv, lora[:, :w["n_lora"]]], axis=-1)
    return x, k, v, _state_from_bd(s_bd), p_last


def _closing_gain(wts, l):
    last = l == wts["w_in"].shape[0] - 1
    return wts["norm_final"][None] if last and l % 2 == 1 else None


def _prompt_group(x_prompt, wts):
    bp, seq, d = x_prompt.shape
    x = x_prompt.reshape(bp * seq, d)
    outs = [[] for _ in range(4)]
    for l in range(wts["w_in"].shape[0]):
        w = _layer_weights(l, BF16, *(wts[n] for n in MIXER_WEIGHTS))
        x, k, v, s, sh = _mixer_prompt(x, bp, seq, w)
        for o, t in zip(outs, (k.reshape(bp, seq, A_HEADS, HEAD_DIM), v.reshape(bp, seq, A_HEADS, HEAD_DIM), s, sh)):
            o.append(t)
        gain = wts["norm_ffn"][l][None]
        i = l // 2
        bf = lambda t: t[i].astype(BF16)
        if l % 2 == 0:
            x = _ffn(x, gain, bf(wts["ffn_w1"]), bf(wts["ffn_w3"]), bf(wts["ffn_w2"]))
        else:
            x = _moe(x, gain, wts["router"][i], wts["moe_w1"][i], wts["moe_w3"][i], wts["moe_w2"][i], 256,
                     final_gain=_closing_gain(wts, l))
    y = x if _closing_gain(wts, l) is not None else _final_norm(x, wts["norm_final"][None])
    return (y.reshape(bp, seq, d),) + tuple(jnp.stack(o) for o in outs)


def _sample_group(x_sample, cache_k, cache_v, state_wkv, state_shift, page_table, wts):
    bs, dec_seq, d = x_sample.shape
    depth, n_pool, page = cache_k.shape[:3]
    assert dec_seq == 1 and 2 * page == MOBA_BLOCK
    cache_kt = cache_k.transpose(0, 1, 3, 4, 2).reshape(depth, n_pool, A_WIDTH, page)
    cache_vt = cache_v.transpose(0, 1, 3, 4, 2).reshape(depth, n_pool, A_WIDTH, page)
    x = x_sample.reshape(bs, d)
    outs = [[] for _ in range(4)]
    for l in range(depth):
        w = _layer_weights(l, F32, *(wts[n] for n in MIXER_WEIGHTS))
        x, k, v, s, sh = _mixer_sample(x, page_table, cache_kt, cache_vt, l, state_wkv[l], state_shift[l], w)
        for o, t in zip(outs, (k.reshape(bs, 1, A_HEADS, HEAD_DIM), v.reshape(bs, 1, A_HEADS, HEAD_DIM), s, sh)):
            o.append(t)
        gain = wts["norm_ffn"][l][None]
        i = l // 2
        if l % 2 == 0:
            x = _ffn(x, gain, wts["ffn_w1"][i], wts["ffn_w3"][i], wts["ffn_w2"][i], precise=True)
        else:
            x = _moe(x, gain, wts["router"][i], wts["moe_w1"][i], wts["moe_w3"][i], wts["moe_w2"][i], 32, precise=True,
                     final_gain=_closing_gain(wts, l))
    y = x if _closing_gain(wts, l) is not None else _final_norm(x, wts["norm_final"][None])
    return (y.reshape(bs, 1, d),) + tuple(jnp.stack(o) for o in outs)


def kernel(x_prompt, x_sample, cache_k, cache_v, state_wkv, state_shift, page_table, norm_attn, w_in, w_up_a, mu_shift, w0, w_dec2, a0, w_a2, w_g2, k_k, k_a, r_k, ln_x_w, ln_x_b, w_up_b, w_out, norm_ffn, ffn_w1, ffn_w3, ffn_w2, router, moe_w1, moe_w3, moe_w2, norm_final):
    assert x_prompt.shape[1] % MOBA_BLOCK == 0
    wts = dict(norm_attn=norm_attn, w_in=w_in, w_up_a=w_up_a, mu_shift=mu_shift, w0=w0, w_dec2=w_dec2, a0=a0, w_a2=w_a2,
               w_g2=w_g2, k_k=k_k, k_a=k_a, r_k=r_k, ln_x_w=ln_x_w, ln_x_b=ln_x_b, w_up_b=w_up_b, w_out=w_out,
               norm_ffn=norm_ffn, ffn_w1=ffn_w1, ffn_w3=ffn_w3, ffn_w2=ffn_w2, router=router, moe_w1=moe_w1,
               moe_w3=moe_w3, moe_w2=moe_w2, norm_final=norm_final)
    y_p, k_p, v_p, s_p, sh_p = _prompt_group(x_prompt, wts)
    y_s, k_s, v_s, s_s, sh_s = _sample_group(x_sample, cache_k, cache_v, state_wkv, state_shift, page_table, wts)
    return (y_p, y_s, k_p, v_p, s_p, sh_p, k_s, v_s, s_s, sh_s)
```

```python
import functools

import jax
import jax.numpy as jnp
from jax import lax
from jax.experimental import pallas as pl
from jax.experimental.pallas import tpu as pltpu

F32 = jnp.float32
BF16 = jnp.bfloat16

A_HEADS = 8
HEAD_DIM = 64
A_WIDTH = 512
R_WIDTH = 512
MOBA_BLOCK = 256
MOBA_TOPK = 3
LORA_DECAY = 64
LORA_ICLR = 64
LORA_GATE = 160
LORA_PAD = 384
RKV_W = 3 * R_WIDTH
GN_EPS = 64e-5
RMS_EPS = 1e-6
MOE_TOPK = 2
RWKV_CHUNK = 64
NEG = -1e30
Q_SCALE = HEAD_DIM ** -0.5 * 1.4426950408889634
V_AUG = HEAD_DIM + 16
VMEM_LIMIT = 56 * 1024 * 1024


def _dot(a, b):
    return jnp.dot(a, b, preferred_element_type=F32)


def _dot_nt(a, b):
    return lax.dot_general(a, b, (((1,), (1,)), ((), ())), preferred_element_type=F32)


def _dot_tn(a, b):
    return lax.dot_general(a, b, (((0,), (0,)), ((), ())), preferred_element_type=F32)


def _mxu(a, b, precise):
    if precise:
        return jnp.dot(a.astype(F32), b.astype(F32), precision=lax.Precision.HIGHEST, preferred_element_type=F32)
    return jnp.dot(a.astype(BF16), b.astype(BF16), preferred_element_type=F32)


def _split2(x):
    hi = x.astype(BF16)
    lo = (x - hi.astype(F32)).astype(BF16)
    return hi, lo


def _split3(x):
    hi = x.astype(BF16)
    r1 = x - hi.astype(F32)
    mid = r1.astype(BF16)
    lo = (r1 - mid.astype(F32)).astype(BF16)
    return hi, mid, lo


def _dot_x2(x, m):
    hi, lo = _split2(x)
    return _dot(hi, m) + _dot(lo, m)


def _rms(x, gain):
    return x * lax.rsqrt(jnp.mean(x * x, axis=-1, keepdims=True) + RMS_EPS) * gain


def _cparams(sem):
    return pltpu.CompilerParams(dimension_semantics=sem, vmem_limit_bytes=VMEM_LIMIT)


def _proj_kernel(x_ref, gain_ref, wqkv_ref, wrkv_ref, wlora_ref, q_ref, *outs, precise):
    k_ref, v_ref, rkv_ref, lora_ref = outs[-4:]
    h = _rms(x_ref[...], gain_ref[...])
    h = h if precise else h.astype(BF16)
    q = _mxu(h, wqkv_ref[:, 0:A_WIDTH], precise) * Q_SCALE
    k = _mxu(h, wqkv_ref[:, A_WIDTH:2 * A_WIDTH], precise)
    v = _mxu(h, wqkv_ref[:, 2 * A_WIDTH:3 * A_WIDTH], precise)
    k_ref[...] = k
    v_ref[...] = v
    rkv_ref[...] = _mxu(h, wrkv_ref[...], precise)
    lora_ref[...] = _mxu(h, wlora_ref[...], precise)
    if precise:
        q_ref[...] = q
        return
    kb_ref, vb_ref = outs[:2]
    q_ref[0] = q.T.astype(BF16)
    kb_ref[...] = k.astype(BF16)
    ones = jnp.ones((V_AUG - HEAD_DIM, MOBA_BLOCK), F32)
    for u in range(v.shape[0] // MOBA_BLOCK):
        vt = v[u * MOBA_BLOCK:(u + 1) * MOBA_BLOCK].T
        parts = []
        for hd in range(A_HEADS):
            parts += [vt[hd * HEAD_DIM:(hd + 1) * HEAD_DIM], ones]
        vb_ref[0, u] = jnp.concatenate(parts, axis=0).astype(BF16)


def _proj(x, gain, wqkv, wrkv, wlora, seq=None, precise=False):
    n, d = x.shape
    tm = min(512, n)
    row = lambda w: pl.BlockSpec((tm, w), lambda i: (i, 0))
    full = lambda a: pl.BlockSpec(a.shape, lambda i: (0, 0))
    if precise:
        attn_specs = [row(A_WIDTH)]
        attn_shapes = [jax.ShapeDtypeStruct((n, A_WIDTH), F32)]
    else:
        tps = seq // tm
        nbt = tm // MOBA_BLOCK
        attn_specs = [pl.BlockSpec((1, A_WIDTH, tm), lambda i: (i // tps, 0, i % tps)), row(A_WIDTH),
                      pl.BlockSpec((1, nbt, A_HEADS * V_AUG, MOBA_BLOCK), lambda i: (i // tps, i % tps, 0, 0))]
        attn_shapes = [jax.ShapeDtypeStruct((n // seq, A_WIDTH, seq), BF16), jax.ShapeDtypeStruct((n, A_WIDTH), BF16),
                       jax.ShapeDtypeStruct((n // seq, seq // MOBA_BLOCK, A_HEADS * V_AUG, MOBA_BLOCK), BF16)]
    return pl.pallas_call(
        functools.partial(_proj_kernel, precise=precise),
        grid=(n // tm,),
        in_specs=[row(d), full(gain), full(wqkv), full(wrkv), full(wlora)],
        out_specs=attn_specs + [row(A_WIDTH)] * 2 + [row(RKV_W), row(LORA_PAD)],
        out_shape=attn_shapes + [jax.ShapeDtypeStruct((n, A_WIDTH), F32)] * 2
        + [jax.ShapeDtypeStruct((n, RKV_W), F32), jax.ShapeDtypeStruct((n, LORA_PAD), F32)],
        compiler_params=_cparams(("parallel",)),
        name="proj",
    )(x, gain, wqkv, wrkv, wlora)


def _kmean_kernel(k_ref, o_ref):
    g = o_ref.shape[0]
    x = k_ref[...].reshape(g, MOBA_BLOCK, A_WIDTH)
    o_ref[...] = jnp.sum(x, axis=1) * (1.0 / MOBA_BLOCK)


def _kmean(k):
    n = k.shape[0]
    nblk = n // MOBA_BLOCK
    g = min(8, nblk)
    return pl.pallas_call(
        _kmean_kernel,
        grid=(nblk // g,),
        in_specs=[pl.BlockSpec((g * MOBA_BLOCK, A_WIDTH), lambda i: (i, 0))],
        out_specs=pl.BlockSpec((g, A_WIDTH), lambda i: (i, 0)),
        out_shape=jax.ShapeDtypeStruct((nblk, A_WIDTH), F32),
        compiler_params=_cparams(("parallel",)),
        name="kmean",
    )(k)


def _moba_kernel(qt_ref, k_ref, va_ref, km_ref, o_ref, bias_ref, s_scr, *, nb, unroll):
    i = pl.program_id(2)
    blk = MOBA_BLOCK
    qt = qt_ref[0]
    row = lax.broadcasted_iota(jnp.int32, qt.shape, 0)
    km_hi, km_lo = _split2(km_ref[0])
    bid = lax.broadcasted_iota(jnp.int32, (nb, blk), 0)
    qms = []
    for hh in range(2):
        qm = jnp.where((row >= hh * HEAD_DIM) & (row < (hh + 1) * HEAD_DIM), qt, jnp.zeros_like(qt))
        qms.append(qm)
        gate = _dot(km_hi, qm) + _dot(km_lo, qm)
        gate = jnp.where(bid < i, gate, -jnp.inf)
        rank = jnp.zeros((nb, blk), F32)
        for m in range(nb):
            gm = gate[m:m + 1, :]
            beats = (gm > gate) | ((gm == gate) & (bid > m))
            rank = rank + jnp.where(beats, 1.0, 0.0)
        sel = (bid < i) & (rank < float(MOBA_TOPK))
        bias_ref[hh] = jnp.where(sel, 0.0, NEG)

    kio = lax.broadcasted_iota(jnp.int32, (blk, blk), 0)
    qio = lax.broadcasted_iota(jnp.int32, (blk, blk), 1)
    causal = kio <= qio
    kd = k_ref[0, i]
    ms = []
    for hh in range(2):
        s = jnp.where(causal, _dot(kd, qms[hh]), NEG)
        s_scr[hh, nb] = s
        ms.append(jnp.max(s, axis=0, keepdims=True))
    n_trips = (i + unroll - 1) // unroll

    def scores(jj, ms):
        ms = list(ms)
        for u in range(unroll):
            j = unroll * jj + u
            kj = k_ref[0, j]
            for hh in range(2):
                s = _dot(kj, qms[hh]) + bias_ref[hh, pl.ds(j, 1), :]
                s_scr[hh, j] = s
                ms[hh] = jnp.maximum(ms[hh], jnp.max(s, axis=0, keepdims=True))
        return tuple(ms)

    ms = lax.fori_loop(0, n_trips, scores, tuple(ms))

    vd = va_ref[0, i]
    accs = []
    for hh in range(2):
        p = jnp.exp2(s_scr[hh, nb] - ms[hh]).astype(BF16)
        accs.append(_dot(vd[hh * V_AUG:(hh + 1) * V_AUG, :], p))

    def weighted(jj, accs):
        accs = list(accs)
        for u in range(unroll):
            j = unroll * jj + u
            vj = va_ref[0, j]
            for hh in range(2):
                p = jnp.exp2(s_scr[hh, j] - ms[hh]).astype(BF16)
                accs[hh] = accs[hh] + _dot(vj[hh * V_AUG:(hh + 1) * V_AUG, :], p)
        return tuple(accs)

    accs = lax.fori_loop(0, n_trips, weighted, tuple(accs))
    o = jnp.concatenate([a[0:HEAD_DIM] / a[HEAD_DIM:HEAD_DIM + 1] for a in accs], axis=0)
    o_ref[0] = o.T.astype(BF16)


def _moba_prompt(qt, kb, va, kmean, bsz, seq):
    nb = seq // MOBA_BLOCK
    k4 = kb.reshape(bsz, nb, MOBA_BLOCK, A_WIDTH)
    km = kmean.reshape(bsz, nb, A_WIDTH)
    out = pl.pallas_call(
        functools.partial(_moba_kernel, nb=nb, unroll=4 if nb % 4 == 0 else 2),
        grid=(bsz, A_HEADS // 2, nb),
        in_specs=[
            pl.BlockSpec((1, 128, MOBA_BLOCK), lambda b, h, i: (b, h, i)),
            pl.BlockSpec((1, nb, MOBA_BLOCK, 128), lambda b, h, i: (b, 0, 0, h)),
            pl.BlockSpec((1, nb, 2 * V_AUG, MOBA_BLOCK), lambda b, h, i: (b, 0, h, 0)),
            pl.BlockSpec((1, nb, 128), lambda b, h, i: (b, 0, h)),
        ],
        out_specs=pl.BlockSpec((1, MOBA_BLOCK, 128), lambda b, h, i: (b, i, h)),
        out_shape=jax.ShapeDtypeStruct((bsz, seq, A_WIDTH), BF16),
        scratch_shapes=[pltpu.VMEM((2, nb, MOBA_BLOCK), F32), pltpu.VMEM((2, nb + 1, MOBA_BLOCK, MOBA_BLOCK), F32)],
        compiler_params=_cparams(("parallel", "parallel", "arbitrary")),
        name="moba_prompt",
    )(qt, k4, va, km)
    return out.reshape(bsz * seq, A_WIDTH)


def _kmean_paged_kernel(pt_ref, ck_ref, o_ref, buf, sem, *, layer, n_pages):
    b = pl.program_id(0)
    slot = b % 2

    def fetch(seq, to_slot):
        for p in range(n_pages):
            pltpu.make_async_copy(ck_ref.at[layer, pt_ref[seq, p]], buf.at[to_slot, p], sem.at[to_slot, p]).start()

    @pl.when(b == 0)
    def _():
        fetch(0, 0)

    @pl.when(b + 1 < pl.num_programs(0))
    def _():
        fetch(b + 1, 1 - slot)

    nb = n_pages // 2
    lane = lax.broadcasted_iota(jnp.int32, (A_WIDTH, nb), 1)
    res = jnp.zeros((A_WIDTH, nb), F32)
    for n in range(nb):
        for e in range(2):
            pltpu.make_async_copy(ck_ref.at[layer, 0], buf.at[slot, 2 * n + e], sem.at[slot, 2 * n + e]).wait()
        tot = jnp.sum(buf[slot, 2 * n] + buf[slot, 2 * n + 1], axis=-1, keepdims=True)
        res = jnp.where(lane == n, tot * (1.0 / MOBA_BLOCK), res)
    o_ref[0] = res


def _kmean_paged(page_table, cache_kt, layer):
    bs, n_pages = page_table.shape
    page = cache_kt.shape[3]
    nb = n_pages // 2
    return pl.pallas_call(
        functools.partial(_kmean_paged_kernel, layer=layer, n_pages=n_pages),
        grid_spec=pltpu.PrefetchScalarGridSpec(
            num_scalar_prefetch=1,
            grid=(bs,),
            in_specs=[pl.BlockSpec(memory_space=pl.ANY)],
            out_specs=pl.BlockSpec((1, A_WIDTH, nb), lambda b, pt: (b, 0, 0)),
            scratch_shapes=[pltpu.VMEM((2, n_pages, A_WIDTH, page), F32), pltpu.SemaphoreType.DMA((2, n_pages))],
        ),
        out_shape=jax.ShapeDtypeStruct((bs, A_WIDTH, nb), F32),
        compiler_params=_cparams(("arbitrary",)),
        name="kmean_paged",
    )(page_table, cache_kt)


def _select_kernel(q_ref, km_ref, o_ref, *, bs, nb):
    head = lax.broadcasted_iota(jnp.int32, (A_HEADS, A_WIDTH), 0)
    chan = lax.broadcasted_iota(jnp.int32, (A_HEADS, A_WIDTH), 1)
    own = (chan >= head * HEAD_DIM) & (chan < (head + 1) * HEAD_DIM)
    bid = lax.broadcasted_iota(jnp.int32, (A_HEADS, nb), 1)
    bidf = bid.astype(F32)
    lane = lax.broadcasted_iota(jnp.int32, (A_HEADS, 128), 1)

    def body(b, _):
        qm = jnp.where(own, q_ref[pl.ds(b, 1), :], 0.0)
        gate = _mxu(qm, km_ref[b], True)
        rank = jnp.zeros((A_HEADS, nb), F32)
        for m in range(nb):
            gm = gate[:, m:m + 1]
            beats = (gm > gate) | ((gm == gate) & (bid > m))
            rank = rank + jnp.where(beats, 1.0, 0.0)
        res = jnp.zeros((A_HEADS, 128), F32)
        for r in range(MOBA_TOPK):
            idx = jnp.sum(jnp.where(rank == float(r), bidf, 0.0), axis=-1, keepdims=True)
            res = jnp.where(lane == r, idx, res)
        o_ref[b] = res.astype(jnp.int32)
        return 0

    lax.fori_loop(0, bs, body, 0)


def _select(qf, kmean_t):
    bs, _, nb = kmean_t.shape
    return pl.pallas_call(
        functools.partial(_select_kernel, bs=bs, nb=nb),
        out_shape=jax.ShapeDtypeStruct((bs, A_HEADS, 128), jnp.int32),
        compiler_params=pltpu.CompilerParams(vmem_limit_bytes=VMEM_LIMIT),
        name="moba_select",
    )(qf, kmean_t)


def _decode_kernel(pt_ref, sel_ref, q_ref, kn_ref, vn_ref, ck_ref, cv_ref, o_ref, kbuf, vbuf, sem, *, layer, pg):
    t = pl.program_id(0)
    n_seq = pl.num_programs(0) - 1

    def copies(seq, slot):
        out = []
        for h in range(A_HEADS):
            for r in range(MOBA_TOPK):
                blk = sel_ref[seq, h * MOBA_TOPK + r]
                for e in range(2):
                    page = pt_ref[seq, 2 * blk + e]
                    for src, dst, s in ((ck_ref, kbuf, 0), (cv_ref, vbuf, 1)):
                        out.append(pltpu.make_async_copy(src.at[layer, page, pl.ds(h * HEAD_DIM, HEAD_DIM), :],
                                                         dst.at[slot, h, :, pl.ds((2 * r + e) * pg, pg)],
                                                         sem.at[slot, s]))
        return out

    @pl.when(t < n_seq)
    def _():
        for c in copies(t, t % 2):
            c.start()

    @pl.when(t > 0)
    def _():
        slot = (t - 1) % 2
        for c in copies(t - 1, slot):
            c.wait()
        q = q_ref[0]
        kn = kn_ref[0]
        vn = vn_ref[0]
        eye = jnp.where(lax.broadcasted_iota(jnp.int32, (HEAD_DIM, HEAD_DIM), 0)
                        == lax.broadcasted_iota(jnp.int32, (HEAD_DIM, HEAD_DIM), 1), 1.0, 0.0)
        for h in range(A_HEADS):
            qh = q[h:h + 1]
            q_col = jnp.sum(eye * qh, axis=-1, keepdims=True)
            s = jnp.sum(kbuf[slot, h] * q_col, axis=0, keepdims=True)
            s_self = jnp.sum(qh * kn[h:h + 1], axis=-1, keepdims=True)
            m = jnp.maximum(jnp.max(s, axis=-1, keepdims=True), s_self)
            p = jnp.exp2(s - m)
            p_self = jnp.exp2(s_self - m)
            l = jnp.sum(p, axis=-1, keepdims=True) + p_self
            pv_col = jnp.sum(vbuf[slot, h] * p, axis=-1, keepdims=True)
            pv = jnp.sum(eye * pv_col, axis=0, keepdims=True)
            o_ref[0, h:h + 1, :] = (pv + p_self * vn[h:h + 1]) / l


def _decode_attn(page_table, sel, qf, k_new, v_new, cache_kt, cache_vt, layer):
    bs = qf.shape[0]
    pg = cache_kt.shape[3]
    heads = lambda t: t.reshape(bs, A_HEADS, HEAD_DIM)
    row = pl.BlockSpec((1, A_HEADS, HEAD_DIM), lambda t, pt, sl: (jnp.maximum(t - 1, 0), 0, 0))
    gathered = pltpu.VMEM((2, A_HEADS, HEAD_DIM, 2 * MOBA_TOPK * pg), F32)
    out = pl.pallas_call(
        functools.partial(_decode_kernel, layer=layer, pg=pg),
        grid_spec=pltpu.PrefetchScalarGridSpec(
            num_scalar_prefetch=2,
            grid=(bs + 1,),
            in_specs=[row, row, row, pl.BlockSpec(memory_space=pl.ANY), pl.BlockSpec(memory_space=pl.ANY)],
            out_specs=row,
            scratch_shapes=[gathered, gathered, pltpu.SemaphoreType.DMA((2, 2))],
        ),
        out_shape=jax.ShapeDtypeStruct((bs, A_HEADS, HEAD_DIM), F32),
        compiler_params=_cparams(("arbitrary",)),
        name="moba_decode",
    )(page_table, sel, heads(qf), heads(k_new), heads(v_new), cache_kt, cache_vt)
    return out.reshape(bs, A_WIDTH)


def _rwkv_kernel(rkv_ref, lora_ref, prkv_ref, plora_ref, s0_ref, mu_rkv_ref, mu_lora_ref, w0_ref, wdec_ref,
                 a0_ref, wa_ref, wg_ref, kk_ref, ka_ref, rk_ref, lnw_ref, lnb_ref, g_ref, tri_ref,
                 ob_ref, s_ref, c_rkv, c_lora, s_an, s_r, s_b, s_k, s_v, s_ld, s_y, *, T, C):
    i = pl.program_id(1)

    @pl.when(i == 0)
    def _():
        c_rkv[...] = prkv_ref[0]
        c_lora[...] = plora_ref[0]
        s_ref[0] = s0_ref[0]

    x = rkv_ref[0]
    xl = lora_ref[0]
    rowi = lax.broadcasted_iota(jnp.int32, (T, 1), 0)

    def shifted(cur, prev):
        return jnp.where(rowi == 0, prev, pltpu.roll(cur, 1, 0))

    xs = shifted(x, c_rkv[...])
    xls = shifted(xl, c_lora[...])
    c_rkv[...] = x[T - 1:T]
    c_lora[...] = xl[T - 1:T]
    pm = x + (xs - x) * mu_rkv_ref[...]
    pml = xl + (xls - xl) * mu_lora_ref[...]
    r = pm[:, 0:R_WIDTH]
    k = pm[:, R_WIDTH:2 * R_WIDTH]
    v = pm[:, 2 * R_WIDTH:3 * R_WIDTH]
    t01 = pml[:, 0:128]
    dec_arg = w0_ref[...] + _dot(jnp.tanh(t01).astype(BF16), wdec_ref[...])
    sp = jnp.maximum(-dec_arg, 0.0) + jnp.log(1.0 + jnp.exp(-jnp.abs(dec_arg)))
    ld = -jnp.exp(-sp - 0.5)
    a = jax.nn.sigmoid(a0_ref[...] + _dot(t01.astype(BF16), wa_ref[...]))
    g = _dot(jax.nn.sigmoid(pml[:, 128:LORA_PAD]).astype(BF16), wg_ref[...])
    gmat = g_ref[...]
    kk0 = k * kk_ref[...]
    kk = kk0 * lax.rsqrt(jnp.maximum(_dot_x2(kk0 * kk0, gmat), 1e-24))
    kmod = k * (1.0 + (a - 1.0) * ka_ref[...])
    bb = kk * a
    s_an[...] = -kk
    s_r[...] = r
    s_b[...] = bb
    s_k[...] = kmod
    s_v[...] = v
    s_ld[...] = ld

    tri = tri_ref[...]
    lane = lax.broadcasted_iota(jnp.int32, (C, 128), 1)
    in_h0 = lane < HEAD_DIM
    ri = lax.broadcasted_iota(jnp.int32, (2 * C, 2 * C), 0)
    ci = lax.broadcasted_iota(jnp.int32, (2 * C, 2 * C), 1)
    same = (ri >= C) == (ci >= C)
    strict = same & (ri > ci)
    incl = same & (ri >= ci)
    eye = jnp.where(ri == ci, 1.0, 0.0)

    def stack_masked(t):
        return jnp.concatenate([jnp.where(in_h0, t, 0.0), jnp.where(in_h0, 0.0, t)], axis=0)

    def chunk(c, _):
        rs = pl.ds(pl.multiple_of(c * C, C), C)
        ld_c = s_ld[rs, :]
        ld_hi, ld_mid, ld_lo = _split3(ld_c)
        cum = _dot(tri, ld_hi) + _dot(tri, ld_mid) + _dot(tri, ld_lo)
        cum_last = cum[C - 1:C]
        e_in = jnp.exp(cum)
        e_ex = jnp.exp(cum - ld_c)
        e_neg = jnp.exp(-cum)
        e_end = jnp.exp(cum_last - cum)
        g_end = jnp.exp(cum_last)
        at = s_an[rs, :] * e_ex
        rt = s_r[rs, :] * e_in
        bt = s_b[rs, :] * e_neg
        kt = s_k[rs, :] * e_neg
        bg = s_b[rs, :] * e_end
        kg = s_k[rs, :] * e_end
        vv = s_v[rs, :]
        pairs = range(4)
        lss = [slice(p * 128, (p + 1) * 128) for p in pairs]
        bf = lambda xs: [x.astype(BF16) for x in xs]
        ar_b = bf([jnp.concatenate([stack_masked(at[:, ls]), stack_masked(rt[:, ls])], axis=0) for ls in lss])
        bk_b = bf([jnp.concatenate([bt[:, ls], bt[:, ls], kt[:, ls], kt[:, ls]], axis=0) for ls in lss])
        vv_b = bf([stack_masked(vv[:, ls]) for ls in lss])
        bkg_b = bf([jnp.concatenate([stack_masked(bg[:, ls]), stack_masked(kg[:, ls])], axis=0) for ls in lss])
        big = [_dot_nt(a, b) for a, b in zip(ar_b, bk_b)]
        a_ab = [jnp.where(strict, m[0:2 * C, 0:2 * C], 0.0) for m in big]
        a_ak = bf([jnp.where(strict, m[0:2 * C, 2 * C:4 * C], 0.0) for m in big])
        a_rbk = bf([jnp.concatenate([jnp.where(incl, m[2 * C:4 * C, 0:2 * C], 0.0),
                                     jnp.where(incl, m[2 * C:4 * C, 2 * C:4 * C], 0.0)], axis=1) for m in big])
        inv = [eye + a for a in a_ab]
        pw = a_ab
        for _ in range(5):
            pw_b = bf(pw)
            pw = [_dot(x, x) for x in pw_b]
            inv = [iv + _dot(x.astype(BF16), iv.astype(BF16)) for x, iv in zip(pw, inv)]
        st = [s_ref[0, p] for p in pairs]
        from_state = [_dot_nt(a, s.astype(BF16)) for a, s in zip(ar_b, st)]
        rhs = [fs[0:2 * C] + _dot(a, v) for fs, a, v in zip(from_state, a_ak, vv_b)]
        u_b = bf([_dot(iv.astype(BF16), r.astype(BF16)) for iv, r in zip(inv, rhs)])
        uv_b = [jnp.concatenate([u, v], axis=0) for u, v in zip(u_b, vv_b)]
        y_s = [fs[2 * C:4 * C] + _dot(a, uv) for fs, a, uv in zip(from_state, a_rbk, uv_b)]
        for p in pairs:
            s_y[rs, lss[p]] = y_s[p][0:C] + y_s[p][C:2 * C]
            s_ref[0, p] = st[p] * g_end[:, lss[p]] + _dot_tn(uv_b[p], bkg_b[p])
        return 0

    lax.fori_loop(0, T // C, chunk, 0, unroll=True)

    y = s_y[...]
    mean = _dot_x2(y, gmat) * (1.0 / HEAD_DIM)
    d = y - mean
    var = _dot_x2(d * d, gmat) * (1.0 / HEAD_DIM)
    yn = d * lax.rsqrt(var + GN_EPS) * lnw_ref[...] + lnb_ref[...]
    bonus = _dot_x2(r * kmod * rk_ref[...], gmat) * v
    ob_ref[0] = ((yn + bonus) * g).astype(BF16)


def _rwkv(rkv, lora, prev_rkv, prev_lora, s0_bd, wts):
    bsz, seq, _ = rkv.shape
    C = RWKV_CHUNK
    T = min(256, seq)
    head_of = jnp.arange(R_WIDTH, dtype=jnp.int32) // HEAD_DIM
    gmat = (head_of[:, None] == head_of[None, :]).astype(BF16)
    tri = (jnp.arange(C)[:, None] >= jnp.arange(C)[None, :]).astype(BF16)
    consts = list(wts) + [gmat, tri]
    seq_spec = lambda w: pl.BlockSpec((1, T, w), lambda b, i: (b, i, 0))
    per_b = lambda a: pl.BlockSpec((1,) + a.shape[1:], lambda b, i: (b,) + (0,) * (a.ndim - 1))
    full = lambda a: pl.BlockSpec(a.shape, lambda b, i: (0,) * a.ndim)
    big = lambda: pltpu.VMEM((T, R_WIDTH), F32)
    return pl.pallas_call(
        functools.partial(_rwkv_kernel, T=T, C=C),
        grid=(bsz, seq // T),
        in_specs=[seq_spec(RKV_W), seq_spec(LORA_PAD), per_b(prev_rkv), per_b(prev_lora), per_b(s0_bd)]
        + [full(a) for a in consts],
        out_specs=[seq_spec(R_WIDTH), per_b(s0_bd)],
        out_shape=[jax.ShapeDtypeStruct((bsz, seq, R_WIDTH), BF16), jax.ShapeDtypeStruct(s0_bd.shape, F32)],
        scratch_shapes=[pltpu.VMEM((1, RKV_W), F32), pltpu.VMEM((1, LORA_PAD), F32)] + [big() for _ in range(7)],
        compiler_params=_cparams(("parallel", "arbitrary")),
        name="rwkv",
    )(rkv, lora, prev_rkv, prev_lora, s0_bd, *consts)


def _rwkv_step_kernel(rkv_ref, lora_ref, prkv_ref, plora_ref, s0_ref, mu_rkv_ref, mu_lora_ref, w0_ref, wdec_ref,
                      a0_ref, wa_ref, wg_ref, kk_ref, ka_ref, rk_ref, lnw_ref, lnb_ref,
                      ob_ref, s_ref, p_r, p_w, p_k, p_v, p_kk, p_a, p_g):
    b = pl.program_id(0)

    @pl.when(b == 0)
    def _():
        x = rkv_ref[...]
        xl = lora_ref[...]
        pm = x + (prkv_ref[...] - x) * mu_rkv_ref[...]
        pml = xl + (plora_ref[...] - xl) * mu_lora_ref[...]
        k = pm[:, R_WIDTH:2 * R_WIDTH]
        t01 = pml[:, 0:128]
        dec_arg = w0_ref[...] + _mxu(jnp.tanh(t01), wdec_ref[...], True)
        sp = jnp.maximum(-dec_arg, 0.0) + jnp.log(1.0 + jnp.exp(-jnp.abs(dec_arg)))
        a = jax.nn.sigmoid(a0_ref[...] + _mxu(t01, wa_ref[...], True))
        p_r[...] = pm[:, 0:R_WIDTH]
        p_w[...] = jnp.exp(-jnp.exp(-sp - 0.5))
        p_k[...] = k * (1.0 + (a - 1.0) * ka_ref[...])
        p_v[...] = pm[:, 2 * R_WIDTH:3 * R_WIDTH]
        p_kk[...] = k * kk_ref[...]
        p_a[...] = a
        p_g[...] = _mxu(jax.nn.sigmoid(pml[:, 128:LORA_PAD]), wg_ref[...], True)
        ob_ref[...] = jnp.zeros(ob_ref.shape, F32)

    rows = pl.ds(pl.multiple_of((b // 8) * 8, 8), 8)
    mine = lax.broadcasted_iota(jnp.int32, (8, 128), 0) == b % 8
    ri = lax.broadcasted_iota(jnp.int32, (128, 128), 0)
    ci = lax.broadcasted_iota(jnp.int32, (128, 128), 1)
    same_head = (ri >= HEAD_DIM) == (ci >= HEAD_DIM)
    eye = jnp.where(ri == ci, 1.0, 0.0)
    top = lax.broadcasted_iota(jnp.int32, (128, 1), 0) < HEAD_DIM
    left = lax.broadcasted_iota(jnp.int32, (1, 128), 1) < HEAD_DIM

    def per_head_rows(t):
        s0 = jnp.sum(jnp.where(left, t, 0.0), axis=-1, keepdims=True)
        s1 = jnp.sum(jnp.where(left, 0.0, t), axis=-1, keepdims=True)
        return jnp.where(left, s0, s1)

    def per_head_cols(t):
        s0 = jnp.sum(jnp.where(top, t, 0.0), axis=0, keepdims=True)
        s1 = jnp.sum(jnp.where(top, 0.0, t), axis=0, keepdims=True)
        return jnp.where(top, s0, s1)

    for p in range(4):
        ls = slice(p * 128, (p + 1) * 128)
        r, w, k, v, kk0, a, g = (jnp.sum(jnp.where(mine, t[rows, ls], 0.0), axis=0, keepdims=True)
                                 for t in (p_r, p_w, p_k, p_v, p_kk, p_a, p_g))
        kk = kk0 * lax.rsqrt(jnp.maximum(per_head_rows(kk0 * kk0), 1e-24))
        st = s0_ref[0, p]
        sa = jnp.sum(st * kk, axis=-1, keepdims=True)
        v_col = jnp.sum(eye * v, axis=-1, keepdims=True)
        st = st * w + jnp.where(same_head, v_col * k - sa * (kk * a), 0.0)
        s_ref[0, p] = st
        y = jnp.sum(st * r, axis=-1, keepdims=True)
        d = y - per_head_cols(y) * (1.0 / HEAD_DIM)
        yn = d * lax.rsqrt(per_head_cols(d * d) * (1.0 / HEAD_DIM) + GN_EPS)
        yn_row = jnp.sum(eye * yn, axis=0, keepdims=True)
        bonus = per_head_rows(r * k * rk_ref[:, ls]) * v
        out = (yn_row * lnw_ref[:, ls] + lnb_ref[:, ls] + bonus) * g
        ob_ref[rows, ls] = jnp.where(mine, out, ob_ref[rows, ls])


def _rwkv_step(rkv, lora, prev_rkv, prev_lora, s0_bd, wts):
    bs = rkv.shape[0]
    full = lambda a: pl.BlockSpec(a.shape, lambda b: (0,) * a.ndim)
    state = pl.BlockSpec((1,) + s0_bd.shape[1:], lambda b: (b, 0, 0, 0))
    args = [rkv, lora, prev_rkv, prev_lora]
    return pl.pallas_call(
        _rwkv_step_kernel,
        grid=(bs,),
        in_specs=[full(a) for a in args] + [state] + [full(a) for a in wts],
        out_specs=[pl.BlockSpec((bs, R_WIDTH), lambda b: (0, 0)), state],
        out_shape=[jax.ShapeDtypeStruct((bs, R_WIDTH), F32), jax.ShapeDtypeStruct(s0_bd.shape, F32)],
        scratch_shapes=[pltpu.VMEM((bs, R_WIDTH), F32) for _ in range(7)],
        compiler_params=_cparams(("arbitrary",)),
        name="rwkv_step",
    )(*args, s0_bd, *wts)


def _state_to_bd(s):
    b = s.shape[0]
    s = s.reshape(b, 4, 2, HEAD_DIM, HEAD_DIM)
    z = jnp.zeros_like(s[:, :, 0])
    top = jnp.concatenate([s[:, :, 0], z], axis=-1)
    bot = jnp.concatenate([z, s[:, :, 1]], axis=-1)
    return jnp.concatenate([top, bot], axis=-2)


def _state_from_bd(s):
    b = s.shape[0]
    h0 = s[:, :, :HEAD_DIM, :HEAD_DIM]
    h1 = s[:, :, HEAD_DIM:, HEAD_DIM:]
    return jnp.stack([h0, h1], axis=2).reshape(b, 8, HEAD_DIM, HEAD_DIM)


def _merge_kernel(x_ref, gain_ref, wg_ref, oa_ref, ob_ref, wa_ref, wb_ref, wo_ref, o_ref, *, precise):
    x = x_ref[...]
    d = x.shape[1]
    gates = jax.nn.sigmoid(_mxu(_rms(x, gain_ref[...]), wg_ref[...], precise))
    ya = _mxu(oa_ref[...], wa_ref[...], precise)
    yb = _mxu(ob_ref[...], wb_ref[...], precise)
    merged = gates[:, :d] * ya + gates[:, d:] * yb
    o_ref[...] = x + _mxu(merged, wo_ref[...], precise)


def _merge(x, gain, wg, oa, ob, wa, wb, wo, precise=False):
    n, d = x.shape
    tm = min(512, n)
    row = lambda w: pl.BlockSpec((tm, w), lambda i: (i, 0))
    full = lambda a: pl.BlockSpec(a.shape, lambda i: (0, 0))
    return pl.pallas_call(
        functools.partial(_merge_kernel, precise=precise),
        grid=(n // tm,),
        in_specs=[row(d), full(gain), full(wg), row(A_WIDTH), row(R_WIDTH), full(wa), full(wb), full(wo)],
        out_specs=row(d),
        out_shape=jax.ShapeDtypeStruct((n, d), F32),
        compiler_params=_cparams(("parallel",)),
        name="merge",
    )(x, gain, wg, oa, ob, wa, wb, wo)


def _ffn_kernel(x_ref, gain_ref, w1_ref, w3_ref, w2_ref, o_ref, h_scr, acc, *, precise):
    f = pl.program_id(1)

    @pl.when(f == 0)
    def _():
        h_scr[...] = _rms(x_ref[...], gain_ref[...]).astype(h_scr.dtype)
        acc[...] = x_ref[...]

    h = h_scr[...]
    t = jax.nn.silu(_mxu(h, w1_ref[...], precise)) * _mxu(h, w3_ref[...], precise)
    acc[...] += _mxu(t, w2_ref[...], precise)

    @pl.when(f == pl.num_programs(1) - 1)
    def _():
        o_ref[...] = acc[...]


def _ffn(x, gain, w1, w3, w2, precise=False):
    n, d = x.shape
    dff = w1.shape[1]
    tm = min(512, n)
    tf = dff // 2
    return pl.pallas_call(
        functools.partial(_ffn_kernel, precise=precise),
        grid=(n // tm, dff // tf),
        in_specs=[pl.BlockSpec((tm, d), lambda i, f: (i, 0)), pl.BlockSpec((1, d), lambda i, f: (0, 0)),
                  pl.BlockSpec((d, tf), lambda i, f: (0, f)), pl.BlockSpec((d, tf), lambda i, f: (0, f)),
                  pl.BlockSpec((tf, d), lambda i, f: (f, 0))],
        out_specs=pl.BlockSpec((tm, d), lambda i, f: (i, 0)),
        out_shape=jax.ShapeDtypeStruct((n, d), F32),
        scratch_shapes=[pltpu.VMEM((tm, d), F32 if precise else BF16), pltpu.VMEM((tm, d), F32)],
        compiler_params=_cparams(("parallel", "arbitrary")),
        name="ffn",
    )(x, gain, w1, w3, w2)


def _router_kernel(x_ref, gain_ref, wr_ref, o_ref, *, n_exp):
    logits = _mxu(_rms(x_ref[...], gain_ref[...]), wr_ref[...], True)
    lane = lax.broadcasted_iota(jnp.int32, logits.shape, 1)
    lanef = lane.astype(F32)
    logits = jnp.where(lane < n_exp, logits, -jnp.inf)
    m1 = jnp.max(logits, axis=-1, keepdims=True)
    e1 = jnp.min(jnp.where(logits == m1, lanef, 1e9), axis=-1, keepdims=True)
    rest = jnp.where(lanef == e1, -jnp.inf, logits)
    m2 = jnp.max(rest, axis=-1, keepdims=True)
    e2 = jnp.min(jnp.where(rest == m2, lanef, 1e9), axis=-1, keepdims=True)
    z = jnp.exp(m2 - m1)
    g1 = 1.0 / (1.0 + z)
    g2 = z / (1.0 + z)
    res = jnp.where(lane == 0, e1, jnp.where(lane == 1, e2, jnp.where(lane == 2, g1, jnp.where(lane == 3, g2, 0.0))))
    o_ref[...] = res[:, 0:8]


def _router(x, gain, w_router):
    n, d = x.shape
    n_exp = w_router.shape[1]
    tm = min(512, n)
    wp = jnp.zeros((d, 128), F32).at[:, :n_exp].set(w_router)
    return pl.pallas_call(
        functools.partial(_router_kernel, n_exp=n_exp),
        grid=(n // tm,),
        in_specs=[pl.BlockSpec((tm, d), lambda i: (i, 0)), pl.BlockSpec((1, d), lambda i: (0, 0)),
                  pl.BlockSpec((d, 128), lambda i: (0, 0))],
        out_specs=pl.BlockSpec((tm, 8), lambda i: (i, 0)),
        out_shape=jax.ShapeDtypeStruct((n, 8), F32),
        compiler_params=_cparams(("parallel",)),
        name="router",
    )(x, gain, wp)


def _experts_kernel(be_ref, xb_ref, gain_ref, w1_ref, w3_ref, w2_ref, o_ref, *wb, precise):
    h = _rms(xb_ref[...], gain_ref[...])
    if precise:
        t = jax.nn.silu(_mxu(h, w1_ref[0], True)) * _mxu(h, w3_ref[0], True)
        o_ref[...] = _mxu(t, w2_ref[0], True)
        return
    i = pl.program_id(0)
    w1b, w3b, w2b = wb

    @pl.when((i == 0) | (be_ref[i] != be_ref[jnp.maximum(i - 1, 0)]))
    def _():
        w1b[...] = w1_ref[0].astype(BF16)
        w3b[...] = w3_ref[0].astype(BF16)
        w2b[...] = w2_ref[0].astype(BF16)

    h = h.astype(BF16)
    t = jax.nn.silu(_dot(h, w1b[...])) * _dot(h, w3b[...])
    o_ref[...] = _dot(t.astype(BF16), w2b[...])


def _experts(blk_e, xb, gain, w1, w3, w2, rb, precise):
    r, d = xb.shape
    dffe = w1.shape[2]
    wspec = lambda shape: pl.BlockSpec((1,) + shape, lambda i, be: (be[i], 0, 0), pipeline_mode=pl.Buffered(1))
    scratch = [] if precise else [pltpu.VMEM((d, dffe), BF16), pltpu.VMEM((d, dffe), BF16), pltpu.VMEM((dffe, d), BF16)]
    return pl.pallas_call(
        functools.partial(_experts_kernel, precise=precise),
        grid_spec=pltpu.PrefetchScalarGridSpec(
            num_scalar_prefetch=1,
            grid=(r // rb,),
            in_specs=[pl.BlockSpec((rb, d), lambda i, be: (i, 0)), pl.BlockSpec((1, d), lambda i, be: (0, 0)),
                      wspec((d, dffe)), wspec((d, dffe)), wspec((dffe, d))],
            out_specs=pl.BlockSpec((rb, d), lambda i, be: (i, 0)),
            scratch_shapes=scratch,
        ),
        out_shape=jax.ShapeDtypeStruct((r, d), F32),
        compiler_params=_cparams(("arbitrary",)),
        name="experts",
    )(blk_e, xb, gain, w1, w3, w2)


def _combine_kernel(x_ref, y0_ref, y1_ref, route_ref, *rest, final):
    g = route_ref[...]
    out = x_ref[...] + g[:, MOE_TOPK:MOE_TOPK + 1] * y0_ref[...] + g[:, MOE_TOPK + 1:MOE_TOPK + 2] * y1_ref[...]
    if final:
        gain_ref, o_ref = rest
        o_ref[...] = _rms(out, gain_ref[...])
    else:
        rest[0][...] = out


def _combine(x, y0, y1, route, final_gain):
    n, d = x.shape
    tm = min(512, n)
    row = pl.BlockSpec((tm, d), lambda i: (i, 0))
    extra = [] if final_gain is None else [final_gain]
    return pl.pallas_call(
        functools.partial(_combine_kernel, final=final_gain is not None),
        grid=(n // tm,),
        in_specs=[row, row, row, pl.BlockSpec((tm, 8), lambda i: (i, 0))] + [pl.BlockSpec((1, d), lambda i: (0, 0))] * len(extra),
        out_specs=row,
        out_shape=jax.ShapeDtypeStruct((n, d), F32),
        compiler_params=_cparams(("parallel",)),
        name="moe_combine",
    )(x, y0, y1, route, *extra)


def _moe(x, gain, w_router, w1, w3, w2, rb, precise=False, final_gain=None, anchor=None):
    n, d = x.shape
    n_exp = w_router.shape[1]
    route = _router(x, gain, w_router)
    top_e = route[:, 0:MOE_TOPK].astype(jnp.int32)
    na = n * MOE_TOPK
    e_flat = top_e.reshape(na)
    onehot = (e_flat[:, None] == jnp.arange(n_exp, dtype=jnp.int32)[None, :]).astype(jnp.int32)
    within = jnp.cumsum(onehot, axis=0) - onehot
    counts = jnp.sum(onehot, axis=0)
    padded = (counts + rb - 1) // rb * rb
    pend = jnp.cumsum(padded)
    pstart = pend - padded
    dest = pstart[e_flat] + jnp.sum(within * onehot, axis=1)
    r = (na + n_exp * (rb - 1) + rb - 1) // rb * rb
    tok = jnp.arange(na, dtype=jnp.int32) // MOE_TOPK
    row_tok = jnp.zeros((r,), jnp.int32).at[dest].set(tok, unique_indices=True)
    blk_start = jnp.arange(r // rb, dtype=jnp.int32) * rb
    blk_e = jnp.minimum(jnp.searchsorted(pend, blk_start, side="right"), n_exp - 1).astype(jnp.int32)
    xb = x[row_tok]
    yield
    if anchor is not None and anchor:
        blk_e, _ = lax.optimization_barrier((blk_e, anchor[-1]))
    yb = _experts(blk_e, xb, gain, w1, w3, w2, rb, precise)
    pos = dest.reshape(n, MOE_TOPK)
    y0, y1 = yb[pos[:, 0]], yb[pos[:, 1]]
    yield
    return _combine(x, y0, y1, route, final_gain)


def _norm_kernel(x_ref, gain_ref, o_ref):
    o_ref[...] = _rms(x_ref[...], gain_ref[...])


def _final_norm(x, gain):
    n, d = x.shape
    tm = min(1024, n)
    return pl.pallas_call(
        _norm_kernel,
        grid=(n // tm,),
        in_specs=[pl.BlockSpec((tm, d), lambda i: (i, 0)), pl.BlockSpec((1, d), lambda i: (0, 0))],
        out_specs=pl.BlockSpec((tm, d), lambda i: (i, 0)),
        out_shape=jax.ShapeDtypeStruct((n, d), F32),
        compiler_params=_cparams(("parallel",)),
        name="final_norm",
    )(x, gain)


MIXER_WEIGHTS = ("norm_attn", "w_in", "w_up_a", "mu_shift", "w0", "w_dec2", "a0", "w_a2", "w_g2", "k_k", "k_a", "r_k",
                 "ln_x_w", "ln_x_b", "w_up_b", "w_out")


def _layer_weights(l, dt, norm_attn, w_in, w_up_a, mu_shift, w0, w_dec2, a0, w_a2, w_g2, k_k, k_a, r_k, ln_x_w, ln_x_b,
                   w_up_b, w_out):
    d = w_in.shape[1]
    wi = w_in[l]
    c0 = 3 * A_WIDTH
    c1 = c0 + RKV_W
    n_lora = LORA_DECAY + LORA_ICLR + LORA_GATE
    c2 = c1 + n_lora
    wqkv = wi[:, :c0].astype(dt)
    wrkv = wi[:, c0:c1].astype(dt)
    wlora = jnp.zeros((d, LORA_PAD), dt).at[:, :n_lora].set(wi[:, c1:c2].astype(dt))
    wgate = wi[:, c2:].astype(dt)
    mu = mu_shift[l]
    mu_rkv = mu[:RKV_W][None]
    mu_lora = jnp.zeros((1, LORA_PAD), F32).at[0, :n_lora].set(mu[RKV_W:])
    wdec = jnp.zeros((128, R_WIDTH), dt).at[:LORA_DECAY].set(w_dec2[l].astype(dt))
    wa = jnp.zeros((128, R_WIDTH), dt).at[LORA_DECAY:LORA_DECAY + LORA_ICLR].set(w_a2[l].astype(dt))
    wg = jnp.zeros((LORA_PAD - 128, R_WIDTH), dt).at[:LORA_GATE].set(w_g2[l].astype(dt))
    rw = (mu_rkv, mu_lora, w0[l][None], wdec, a0[l][None], wa, wg, k_k[l][None], k_a[l][None],
          r_k[l].reshape(1, R_WIDTH), ln_x_w[l][None], ln_x_b[l][None])
    return dict(gain=norm_attn[l][None], wqkv=wqkv, wrkv=wrkv, wlora=wlora, wgate=wgate, rw=rw,
                wa=w_up_a[l].astype(dt), wb=w_up_b[l].astype(dt), wo=w_out[l].astype(dt), n_lora=n_lora)


def _mixer_prompt(x, bsz, seq, w):
    qt, kb, va, k, v, rkv, lora = _proj(x, w["gain"], w["wqkv"], w["wrkv"], w["wlora"], seq=seq)
    oa = _moba_prompt(qt, kb, va, _kmean(k), bsz, seq)
    zeros = lambda width: jnp.zeros((bsz, 1, width), F32)
    ob, s_bd = _rwkv(rkv.reshape(bsz, seq, RKV_W), lora.reshape(bsz, seq, LORA_PAD), zeros(RKV_W), zeros(LORA_PAD),
                     jnp.zeros((bsz, 4, 128, 128), F32), w["rw"])
    x = _merge(x, w["gain"], w["wgate"], oa, ob.reshape(bsz * seq, R_WIDTH), w["wa"], w["wb"], w["wo"])
    last = lambda t: t.reshape(bsz, seq, -1)[:, -1]
    p_last = jnp.concatenate([last(rkv), last(lora)[:, :w["n_lora"]]], axis=-1)
    return x, k, v, _state_from_bd(s_bd), p_last


def _mixer_sample(x, page_table, cache_kt, cache_vt, layer, s0, p_prev, w):
    bs = x.shape[0]
    q, k, v, rkv, lora = _proj(x, w["gain"], w["wqkv"], w["wrkv"], w["wlora"], precise=True)
    kmean_t = _kmean_paged(page_table, cache_kt, layer)
    sel = _select(q, kmean_t)[:, :, :MOBA_TOPK].reshape(bs, A_HEADS * MOBA_TOPK)
    oa = _decode_attn(page_table, sel, q, k, v, cache_kt, cache_vt, layer)
    prev_lora = jnp.zeros((bs, LORA_PAD), F32).at[:, :w["n_lora"]].set(p_prev[:, RKV_W:])
    ob, s_bd = _rwkv_step(rkv, lora, p_prev[:, :RKV_W], prev_lora, _state_to_bd(s0), w["rw"])
    x = _merge(x, w["gain"], w["wgate"], oa, ob, w["wa"], w["wb"], w["wo"], precise=True)
    p_last = jnp.concatenate([rkv, lora[:, :w["n_lora"]]], axis=-1)
    return x, k, v, _state_from_bd(s_bd), p_last


def _closing_gain(wts, l):
    last = l == wts["w_in"].shape[0] - 1
    return wts["norm_final"][None] if last and l % 2 == 1 else None


def _prompt_group(x_prompt, wts, other_group):
    bp, seq, d = x_prompt.shape
    x = x_prompt.reshape(bp * seq, d)
    outs = [[] for _ in range(4)]
    for l in range(wts["w_in"].shape[0]):
        w = _layer_weights(l, BF16, *(wts[n] for n in MIXER_WEIGHTS))
        x, k, v, s, sh = _mixer_prompt(x, bp, seq, w)
        for o, t in zip(outs, (k.reshape(bp, seq, A_HEADS, HEAD_DIM), v.reshape(bp, seq, A_HEADS, HEAD_DIM), s, sh)):
            o.append(t)
        gain = wts["norm_ffn"][l][None]
        i = l // 2
        bf = lambda t: t[i].astype(BF16)
        if l % 2 == 0:
            yield
            x = _ffn(x, gain, bf(wts["ffn_w1"]), bf(wts["ffn_w3"]), bf(wts["ffn_w2"]))
            yield
        else:
            x = yield from _moe(x, gain, wts["router"][i], wts["moe_w1"][i], wts["moe_w3"][i], wts["moe_w2"][i], 256,
                                final_gain=_closing_gain(wts, l), anchor=other_group)
    y = x if _closing_gain(wts, l) is not None else _final_norm(x, wts["norm_final"][None])
    return (y.reshape(bp, seq, d),) + tuple(jnp.stack(o) for o in outs)


def _sample_group(x_sample, cache_k, cache_v, state_wkv, state_shift, page_table, wts, mixed):
    bs, dec_seq, d = x_sample.shape
    depth, n_pool, page = cache_k.shape[:3]
    assert dec_seq == 1 and 2 * page == MOBA_BLOCK
    cache_kt = cache_k.transpose(0, 1, 3, 4, 2).reshape(depth, n_pool, A_WIDTH, page)
    cache_vt = cache_v.transpose(0, 1, 3, 4, 2).reshape(depth, n_pool, A_WIDTH, page)
    x = x_sample.reshape(bs, d)
    outs = [[] for _ in range(4)]
    for l in range(depth):
        w = _layer_weights(l, F32, *(wts[n] for n in MIXER_WEIGHTS))
        x, k, v, s, sh = _mixer_sample(x, page_table, cache_kt, cache_vt, l, state_wkv[l], state_shift[l], w)
        mixed.append(x)
        for o, t in zip(outs, (k.reshape(bs, 1, A_HEADS, HEAD_DIM), v.reshape(bs, 1, A_HEADS, HEAD_DIM), s, sh)):
            o.append(t)
        gain = wts["norm_ffn"][l][None]
        i = l // 2
        if l % 2 == 0:
            yield
            x = _ffn(x, gain, wts["ffn_w1"][i], wts["ffn_w3"][i], wts["ffn_w2"][i], precise=True)
            yield
        else:
            x = yield from _moe(x, gain, wts["router"][i], wts["moe_w1"][i], wts["moe_w3"][i], wts["moe_w2"][i], 32,
                                precise=True, final_gain=_closing_gain(wts, l))
    y = x if _closing_gain(wts, l) is not None else _final_norm(x, wts["norm_final"][None])
    return (y.reshape(bs, 1, d),) + tuple(jnp.stack(o) for o in outs)


def _alternate(*stages):
    results = [None] * len(stages)
    live = list(enumerate(stages))
    while live:
        for item in list(live):
            try:
                next(item[1])
            except StopIteration as done:
                results[item[0]] = done.value
                live.remove(item)
    return results


def kernel(x_prompt, x_sample, cache_k, cache_v, state_wkv, state_shift, page_table, norm_attn, w_in, w_up_a, mu_shift, w0, w_dec2, a0, w_a2, w_g2, k_k, k_a, r_k, ln_x_w, ln_x_b, w_up_b, w_out, norm_ffn, ffn_w1, ffn_w3, ffn_w2, router, moe_w1, moe_w3, moe_w2, norm_final):
    assert x_prompt.shape[1] % MOBA_BLOCK == 0
    wts = dict(norm_attn=norm_attn, w_in=w_in, w_up_a=w_up_a, mu_shift=mu_shift, w0=w0, w_dec2=w_dec2, a0=a0, w_a2=w_a2,
               w_g2=w_g2, k_k=k_k, k_a=k_a, r_k=r_k, ln_x_w=ln_x_w, ln_x_b=ln_x_b, w_up_b=w_up_b, w_out=w_out,
               norm_ffn=norm_ffn, ffn_w1=ffn_w1, ffn_w3=ffn_w3, ffn_w2=ffn_w2, router=router, moe_w1=moe_w1,
               moe_w3=moe_w3, moe_w2=moe_w2, norm_final=norm_final)
    sample_mixed = []
    (y_p, k_p, v_p, s_p, sh_p), (y_s, k_s, v_s, s_s, sh_s) = _alternate(
        _prompt_group(x_prompt, wts, sample_mixed),
        _sample_group(x_sample, cache_k, cache_v, state_wkv, state_shift, page_table, wts, sample_mixed))
    return (y_p, y_s, k_p, v_p, s_p, sh_p, k_s, v_s, s_s, sh_s)
```

```python
import functools

import jax
import jax.numpy as jnp
from jax import lax
from jax.experimental import pallas as pl
from jax.experimental.pallas import tpu as pltpu

F32 = jnp.float32
BF16 = jnp.bfloat16

A_HEADS = 8
HEAD_DIM = 64
A_WIDTH = 512
R_WIDTH = 512
MOBA_BLOCK = 256
MOBA_TOPK = 3
LORA_DECAY = 64
LORA_ICLR = 64
LORA_GATE = 160
LORA_PAD = 384
RKV_W = 3 * R_WIDTH
GN_EPS = 64e-5
RMS_EPS = 1e-6
MOE_TOPK = 2
RWKV_CHUNK = 64
NEG = -1e30
Q_SCALE = HEAD_DIM ** -0.5 * 1.4426950408889634
V_AUG = HEAD_DIM + 16
VMEM_LIMIT = 56 * 1024 * 1024


def _dot(a, b):
    return jnp.dot(a, b, preferred_element_type=F32)


def _dot_nt(a, b):
    return lax.dot_general(a, b, (((1,), (1,)), ((), ())), preferred_element_type=F32)


def _dot_tn(a, b):
    return lax.dot_general(a, b, (((0,), (0,)), ((), ())), preferred_element_type=F32)


def _mxu(a, b, precise):
    if precise:
        return jnp.dot(a.astype(F32), b.astype(F32), precision=lax.Precision.HIGHEST, preferred_element_type=F32)
    return jnp.dot(a.astype(BF16), b.astype(BF16), preferred_element_type=F32)


def _split2(x):
    hi = x.astype(BF16)
    lo = (x - hi.astype(F32)).astype(BF16)
    return hi, lo


def _split3(x):
    hi = x.astype(BF16)
    r1 = x - hi.astype(F32)
    mid = r1.astype(BF16)
    lo = (r1 - mid.astype(F32)).astype(BF16)
    return hi, mid, lo


def _dot_x2(x, m):
    hi, lo = _split2(x)
    return _dot(hi, m) + _dot(lo, m)


def _rms(x, gain):
    return x * lax.rsqrt(jnp.mean(x * x, axis=-1, keepdims=True) + RMS_EPS) * gain


def _cparams(sem):
    return pltpu.CompilerParams(dimension_semantics=sem, vmem_limit_bytes=VMEM_LIMIT)


def _proj_kernel(x_ref, gain_ref, wqkv_ref, wrkv_ref, wlora_ref, q_ref, *outs, precise):
    k_ref, v_ref, rkv_ref, lora_ref = outs[-4:]
    h = _rms(x_ref[...], gain_ref[...])
    h = h if precise else h.astype(BF16)
    q = _mxu(h, wqkv_ref[:, 0:A_WIDTH], precise) * Q_SCALE
    k = _mxu(h, wqkv_ref[:, A_WIDTH:2 * A_WIDTH], precise)
    v = _mxu(h, wqkv_ref[:, 2 * A_WIDTH:3 * A_WIDTH], precise)
    k_ref[...] = k
    v_ref[...] = v
    rkv_ref[...] = _mxu(h, wrkv_ref[...], precise)
    lora_ref[...] = _mxu(h, wlora_ref[...], precise)
    if precise:
        q_ref[...] = q
        return
    kb_ref, vb_ref = outs[:2]
    q_ref[0] = q.T.astype(BF16)
    kb_ref[...] = k.astype(BF16)
    ones = jnp.ones((V_AUG - HEAD_DIM, MOBA_BLOCK), F32)
    for u in range(v.shape[0] // MOBA_BLOCK):
        vt = v[u * MOBA_BLOCK:(u + 1) * MOBA_BLOCK].T
        parts = []
        for hd in range(A_HEADS):
            parts += [vt[hd * HEAD_DIM:(hd + 1) * HEAD_DIM], ones]
        vb_ref[0, u] = jnp.concatenate(parts, axis=0).astype(BF16)


def _proj(x, gain, wqkv, wrkv, wlora, seq=None, precise=False):
    n, d = x.shape
    tm = min(512, n)
    row = lambda w: pl.BlockSpec((tm, w), lambda i: (i, 0))
    full = lambda a: pl.BlockSpec(a.shape, lambda i: (0, 0))
    if precise:
        attn_specs = [row(A_WIDTH)]
        attn_shapes = [jax.ShapeDtypeStruct((n, A_WIDTH), F32)]
    else:
        tps = seq // tm
        nbt = tm // MOBA_BLOCK
        attn_specs = [pl.BlockSpec((1, A_WIDTH, tm), lambda i: (i // tps, 0, i % tps)), row(A_WIDTH),
                      pl.BlockSpec((1, nbt, A_HEADS * V_AUG, MOBA_BLOCK), lambda i: (i // tps, i % tps, 0, 0))]
        attn_shapes = [jax.ShapeDtypeStruct((n // seq, A_WIDTH, seq), BF16), jax.ShapeDtypeStruct((n, A_WIDTH), BF16),
                       jax.ShapeDtypeStruct((n // seq, seq // MOBA_BLOCK, A_HEADS * V_AUG, MOBA_BLOCK), BF16)]
    return pl.pallas_call(
        functools.partial(_proj_kernel, precise=precise),
        grid=(n // tm,),
        in_specs=[row(d), full(gain), full(wqkv), full(wrkv), full(wlora)],
        out_specs=attn_specs + [row(A_WIDTH)] * 2 + [row(RKV_W), row(LORA_PAD)],
        out_shape=attn_shapes + [jax.ShapeDtypeStruct((n, A_WIDTH), F32)] * 2
        + [jax.ShapeDtypeStruct((n, RKV_W), F32), jax.ShapeDtypeStruct((n, LORA_PAD), F32)],
        compiler_params=_cparams(("parallel",)),
        name="proj",
    )(x, gain, wqkv, wrkv, wlora)


def _kmean_kernel(k_ref, o_ref):
    g = o_ref.shape[0]
    x = k_ref[...].reshape(g, MOBA_BLOCK, A_WIDTH)
    o_ref[...] = jnp.sum(x, axis=1) * (1.0 / MOBA_BLOCK)


def _kmean(k):
    n = k.shape[0]
    nblk = n // MOBA_BLOCK
    g = min(8, nblk)
    return pl.pallas_call(
        _kmean_kernel,
        grid=(nblk // g,),
        in_specs=[pl.BlockSpec((g * MOBA_BLOCK, A_WIDTH), lambda i: (i, 0))],
        out_specs=pl.BlockSpec((g, A_WIDTH), lambda i: (i, 0)),
        out_shape=jax.ShapeDtypeStruct((nblk, A_WIDTH), F32),
        compiler_params=_cparams(("parallel",)),
        name="kmean",
    )(k)


def _moba_kernel(qt_ref, k_ref, va_ref, km_ref, o_ref, bias_ref, s_scr, *, nb, unroll):
    blk = MOBA_BLOCK
    tq = 2 * blk
    b0 = 2 * pl.program_id(2)
    qt = qt_ref[0]
    row = lax.broadcasted_iota(jnp.int32, qt.shape, 0)
    km_hi, km_lo = _split2(km_ref[0])
    bid = lax.broadcasted_iota(jnp.int32, (nb, tq), 0)
    past = bid < b0 + (lax.broadcasted_iota(jnp.int32, (nb, tq), 1) >= blk).astype(jnp.int32)
    qms = []
    b0_rows = []
    for hh in range(2):
        qm = jnp.where((row >= hh * HEAD_DIM) & (row < (hh + 1) * HEAD_DIM), qt, jnp.zeros_like(qt))
        qms.append(qm)
        gate = _dot(km_hi, qm) + _dot(km_lo, qm)
        gate = jnp.where(past, gate, -jnp.inf)
        bias = jnp.full((nb, tq), NEG, F32)
        for _ in range(MOBA_TOPK):
            best = jnp.max(gate, axis=0, keepdims=True)
            first = jnp.min(jnp.where(gate == best, bid, nb), axis=0, keepdims=True)
            pick = bid == first
            bias = jnp.where(pick & past, 0.0, bias)
            gate = jnp.where(pick, -jnp.inf, gate)
        b0_rows.append(jnp.sum(jnp.where(bid == b0, bias, 0.0), axis=0, keepdims=True))
        bias_ref[hh] = jnp.where(bid < b0, bias, NEG)

    kio = lax.broadcasted_iota(jnp.int32, (blk, tq), 0)
    qio = lax.broadcasted_iota(jnp.int32, (blk, tq), 1)
    lower = qio < blk
    mask_b1 = jnp.where(~lower & (kio <= qio - blk), 0.0, NEG)
    ms = []
    for hh in range(2):
        mask_b0 = jnp.where(lower, jnp.where(kio <= qio, 0.0, NEG), b0_rows[hh])
        s0 = _dot(k_ref[0, b0], qms[hh]) + mask_b0
        s1 = _dot(k_ref[0, b0 + 1], qms[hh]) + mask_b1
        s_scr[hh, nb] = s0
        s_scr[hh, nb + 1] = s1
        ms.append(jnp.maximum(jnp.max(s0, axis=0, keepdims=True), jnp.max(s1, axis=0, keepdims=True)))
    n_trips = (b0 + unroll - 1) // unroll

    def scores(jj, ms):
        ms = list(ms)
        for u in range(unroll):
            j = unroll * jj + u
            kj = k_ref[0, j]
            for hh in range(2):
                s = _dot(kj, qms[hh]) + bias_ref[hh, pl.ds(j, 1), :]
                s_scr[hh, j] = s
                ms[hh] = jnp.maximum(ms[hh], jnp.max(s, axis=0, keepdims=True))
        return tuple(ms)

    ms = lax.fori_loop(0, n_trips, scores, tuple(ms))

    accs = []
    for hh in range(2):
        acc = None
        for d in range(2):
            p = jnp.exp2(s_scr[hh, nb + d] - ms[hh]).astype(BF16)
            part = _dot(va_ref[0, b0 + d][hh * V_AUG:(hh + 1) * V_AUG, :], p)
            acc = part if acc is None else acc + part
        accs.append(acc)

    def weighted(jj, accs):
        accs = list(accs)
        for u in range(unroll):
            j = unroll * jj + u
            vj = va_ref[0, j]
            for hh in range(2):
                p = jnp.exp2(s_scr[hh, j] - ms[hh]).astype(BF16)
                accs[hh] = accs[hh] + _dot(vj[hh * V_AUG:(hh + 1) * V_AUG, :], p)
        return tuple(accs)

    accs = lax.fori_loop(0, n_trips, weighted, tuple(accs))
    o = jnp.concatenate([a[0:HEAD_DIM] / a[HEAD_DIM:HEAD_DIM + 1] for a in accs], axis=0)
    o_ref[0] = o.T.astype(BF16)


def _moba_prompt(qt, kb, va, kmean, bsz, seq):
    nb = seq // MOBA_BLOCK
    assert nb % 2 == 0
    tq = 2 * MOBA_BLOCK
    k4 = kb.reshape(bsz, nb, MOBA_BLOCK, A_WIDTH)
    km = kmean.reshape(bsz, nb, A_WIDTH)
    out = pl.pallas_call(
        functools.partial(_moba_kernel, nb=nb, unroll=4 if nb % 4 == 0 else 2),
        grid=(bsz, A_HEADS // 2, nb // 2),
        in_specs=[
            pl.BlockSpec((1, 128, tq), lambda b, h, i: (b, h, i)),
            pl.BlockSpec((1, nb, MOBA_BLOCK, 128), lambda b, h, i: (b, 0, 0, h)),
            pl.BlockSpec((1, nb, 2 * V_AUG, MOBA_BLOCK), lambda b, h, i: (b, 0, h, 0)),
            pl.BlockSpec((1, nb, 128), lambda b, h, i: (b, 0, h)),
        ],
        out_specs=pl.BlockSpec((1, tq, 128), lambda b, h, i: (b, i, h)),
        out_shape=jax.ShapeDtypeStruct((bsz, seq, A_WIDTH), BF16),
        scratch_shapes=[pltpu.VMEM((2, nb, tq), F32), pltpu.VMEM((2, nb + 2, MOBA_BLOCK, tq), F32)],
        compiler_params=_cparams(("parallel", "parallel", "arbitrary")),
        name="moba_prompt",
    )(qt, k4, va, km)
    return out.reshape(bsz * seq, A_WIDTH)


def _kmean_paged_kernel(pt_ref, ck_ref, o_ref, buf, sem, *, layer, n_pages):
    b = pl.program_id(0)
    slot = b % 2

    def fetch(seq, to_slot):
        for p in range(n_pages):
            pltpu.make_async_copy(ck_ref.at[layer, pt_ref[seq, p]], buf.at[to_slot, p], sem.at[to_slot, p]).start()

    @pl.when(b == 0)
    def _():
        fetch(0, 0)

    @pl.when(b + 1 < pl.num_programs(0))
    def _():
        fetch(b + 1, 1 - slot)

    nb = n_pages // 2
    lane = lax.broadcasted_iota(jnp.int32, (A_WIDTH, nb), 1)
    res = jnp.zeros((A_WIDTH, nb), F32)
    for n in range(nb):
        for e in range(2):
            pltpu.make_async_copy(ck_ref.at[layer, 0], buf.at[slot, 2 * n + e], sem.at[slot, 2 * n + e]).wait()
        tot = jnp.sum(buf[slot, 2 * n] + buf[slot, 2 * n + 1], axis=-1, keepdims=True)
        res = jnp.where(lane == n, tot * (1.0 / MOBA_BLOCK), res)
    o_ref[0] = res


def _kmean_paged(page_table, cache_kt, layer):
    bs, n_pages = page_table.shape
    page = cache_kt.shape[3]
    nb = n_pages // 2
    return pl.pallas_call(
        functools.partial(_kmean_paged_kernel, layer=layer, n_pages=n_pages),
        grid_spec=pltpu.PrefetchScalarGridSpec(
            num_scalar_prefetch=1,
            grid=(bs,),
            in_specs=[pl.BlockSpec(memory_space=pl.ANY)],
            out_specs=pl.BlockSpec((1, A_WIDTH, nb), lambda b, pt: (b, 0, 0)),
            scratch_shapes=[pltpu.VMEM((2, n_pages, A_WIDTH, page), F32), pltpu.SemaphoreType.DMA((2, n_pages))],
        ),
        out_shape=jax.ShapeDtypeStruct((bs, A_WIDTH, nb), F32),
        compiler_params=_cparams(("arbitrary",)),
        name="kmean_paged",
    )(page_table, cache_kt)


def _select_kernel(q_ref, km_ref, o_ref, *, bs, nb):
    head = lax.broadcasted_iota(jnp.int32, (A_HEADS, A_WIDTH), 0)
    chan = lax.broadcasted_iota(jnp.int32, (A_HEADS, A_WIDTH), 1)
    own = (chan >= head * HEAD_DIM) & (chan < (head + 1) * HEAD_DIM)
    bid = lax.broadcasted_iota(jnp.int32, (A_HEADS, nb), 1)
    bidf = bid.astype(F32)
    lane = lax.broadcasted_iota(jnp.int32, (A_HEADS, 128), 1)

    def body(b, _):
        qm = jnp.where(own, q_ref[pl.ds(b, 1), :], 0.0)
        gate = _mxu(qm, km_ref[b], True)
        rank = jnp.zeros((A_HEADS, nb), F32)
        for m in range(nb):
            gm = gate[:, m:m + 1]
            beats = (gm > gate) | ((gm == gate) & (bid > m))
            rank = rank + jnp.where(beats, 1.0, 0.0)
        res = jnp.zeros((A_HEADS, 128), F32)
        for r in range(MOBA_TOPK):
            idx = jnp.sum(jnp.where(rank == float(r), bidf, 0.0), axis=-1, keepdims=True)
            res = jnp.where(lane == r, idx, res)
        o_ref[b] = res.astype(jnp.int32)
        return 0

    lax.fori_loop(0, bs, body, 0)


def _select(qf, kmean_t):
    bs, _, nb = kmean_t.shape
    return pl.pallas_call(
        functools.partial(_select_kernel, bs=bs, nb=nb),
        out_shape=jax.ShapeDtypeStruct((bs, A_HEADS, 128), jnp.int32),
        compiler_params=pltpu.CompilerParams(vmem_limit_bytes=VMEM_LIMIT),
        name="moba_select",
    )(qf, kmean_t)


def _decode_kernel(pt_ref, sel_ref, q_ref, kn_ref, vn_ref, ck_ref, cv_ref, o_ref, kbuf, vbuf, sem, *, layer, pg):
    t = pl.program_id(0)
    n_seq = pl.num_programs(0) - 1

    def copies(seq, slot):
        out = []
        for h in range(A_HEADS):
            for r in range(MOBA_TOPK):
                blk = sel_ref[seq, h * MOBA_TOPK + r]
                for e in range(2):
                    page = pt_ref[seq, 2 * blk + e]
                    for src, dst, s in ((ck_ref, kbuf, 0), (cv_ref, vbuf, 1)):
                        out.append(pltpu.make_async_copy(src.at[layer, page, pl.ds(h * HEAD_DIM, HEAD_DIM), :],
                                                         dst.at[slot, h, :, pl.ds((2 * r + e) * pg, pg)],
                                                         sem.at[slot, s]))
        return out

    @pl.when(t < n_seq)
    def _():
        for c in copies(t, t % 2):
            c.start()

    @pl.when(t > 0)
    def _():
        slot = (t - 1) % 2
        for c in copies(t - 1, slot):
            c.wait()
        q = q_ref[0]
        kn = kn_ref[0]
        vn = vn_ref[0]
        eye = jnp.where(lax.broadcasted_iota(jnp.int32, (HEAD_DIM, HEAD_DIM), 0)
                        == lax.broadcasted_iota(jnp.int32, (HEAD_DIM, HEAD_DIM), 1), 1.0, 0.0)
        for h in range(A_HEADS):
            qh = q[h:h + 1]
            q_col = jnp.sum(eye * qh, axis=-1, keepdims=True)
            s = jnp.sum(kbuf[slot, h] * q_col, axis=0, keepdims=True)
            s_self = jnp.sum(qh * kn[h:h + 1], axis=-1, keepdims=True)
            m = jnp.maximum(jnp.max(s, axis=-1, keepdims=True), s_self)
            p = jnp.exp2(s - m)
            p_self = jnp.exp2(s_self - m)
            l = jnp.sum(p, axis=-1, keepdims=True) + p_self
            pv_col = jnp.sum(vbuf[slot, h] * p, axis=-1, keepdims=True)
            pv = jnp.sum(eye * pv_col, axis=0, keepdims=True)
            o_ref[0, h:h + 1, :] = (pv + p_self * vn[h:h + 1]) / l


def _decode_attn(page_table, sel, qf, k_new, v_new, cache_kt, cache_vt, layer):
    bs = qf.shape[0]
    pg = cache_kt.shape[3]
    heads = lambda t: t.reshape(bs, A_HEADS, HEAD_DIM)
    row = pl.BlockSpec((1, A_HEADS, HEAD_DIM), lambda t, pt, sl: (jnp.maximum(t - 1, 0), 0, 0))
    gathered = pltpu.VMEM((2, A_HEADS, HEAD_DIM, 2 * MOBA_TOPK * pg), F32)
    out = pl.pallas_call(
        functools.partial(_decode_kernel, layer=layer, pg=pg),
        grid_spec=pltpu.PrefetchScalarGridSpec(
            num_scalar_prefetch=2,
            grid=(bs + 1,),
            in_specs=[row, row, row, pl.BlockSpec(memory_space=pl.ANY), pl.BlockSpec(memory_space=pl.ANY)],
            out_specs=row,
            scratch_shapes=[gathered, gathered, pltpu.SemaphoreType.DMA((2, 2))],
        ),
        out_shape=jax.ShapeDtypeStruct((bs, A_HEADS, HEAD_DIM), F32),
        compiler_params=_cparams(("arbitrary",)),
        name="moba_decode",
    )(page_table, sel, heads(qf), heads(k_new), heads(v_new), cache_kt, cache_vt)
    return out.reshape(bs, A_WIDTH)


def _rwkv_kernel(rkv_ref, lora_ref, prkv_ref, plora_ref, s0_ref, mu_rkv_ref, mu_lora_ref, w0_ref, wdec_ref,
                 a0_ref, wa_ref, wg_ref, kk_ref, ka_ref, rk_ref, lnw_ref, lnb_ref, g_ref, tri_ref,
                 ob_ref, s_ref, c_rkv, c_lora, s_an, s_r, s_b, s_k, s_v, s_ld, s_y, *, T, C):
    i = pl.program_id(1)

    @pl.when(i == 0)
    def _():
        c_rkv[...] = prkv_ref[0]
        c_lora[...] = plora_ref[0]
        s_ref[0] = s0_ref[0]

    x = rkv_ref[0]
    xl = lora_ref[0]
    rowi = lax.broadcasted_iota(jnp.int32, (T, 1), 0)

    def shifted(cur, prev):
        return jnp.where(rowi == 0, prev, pltpu.roll(cur, 1, 0))

    xs = shifted(x, c_rkv[...])
    xls = shifted(xl, c_lora[...])
    c_rkv[...] = x[T - 1:T]
    c_lora[...] = xl[T - 1:T]
    pm = x + (xs - x) * mu_rkv_ref[...]
    pml = xl + (xls - xl) * mu_lora_ref[...]
    r = pm[:, 0:R_WIDTH]
    k = pm[:, R_WIDTH:2 * R_WIDTH]
    v = pm[:, 2 * R_WIDTH:3 * R_WIDTH]
    t01 = pml[:, 0:128]
    dec_arg = w0_ref[...] + _dot(jnp.tanh(t01).astype(BF16), wdec_ref[...])
    sp = jnp.maximum(-dec_arg, 0.0) + jnp.log(1.0 + jnp.exp(-jnp.abs(dec_arg)))
    ld = -jnp.exp(-sp - 0.5)
    a = jax.nn.sigmoid(a0_ref[...] + _dot(t01.astype(BF16), wa_ref[...]))
    g = _dot(jax.nn.sigmoid(pml[:, 128:LORA_PAD]).astype(BF16), wg_ref[...])
    gmat = g_ref[...]

    def head_sums(t):
        return jnp.concatenate([_dot_x2(t[:, p * 128:(p + 1) * 128], gmat) for p in range(4)], axis=1)

    kk0 = k * kk_ref[...]
    kk = kk0 * lax.rsqrt(jnp.maximum(head_sums(kk0 * kk0), 1e-24))
    kmod = k * (1.0 + (a - 1.0) * ka_ref[...])
    bb = kk * a
    s_an[...] = -kk
    s_r[...] = r
    s_b[...] = bb
    s_k[...] = kmod
    s_v[...] = v
    s_ld[...] = ld

    tri = tri_ref[...]
    lane = lax.broadcasted_iota(jnp.int32, (C, 128), 1)
    in_h0 = lane < HEAD_DIM
    ri = lax.broadcasted_iota(jnp.int32, (2 * C, 2 * C), 0)
    ci = lax.broadcasted_iota(jnp.int32, (2 * C, 2 * C), 1)
    same = (ri >= C) == (ci >= C)
    strict = same & (ri > ci)
    incl = same & (ri >= ci)
    eye = jnp.where(ri == ci, 1.0, 0.0)

    def stack_masked(t):
        return jnp.concatenate([jnp.where(in_h0, t, 0.0), jnp.where(in_h0, 0.0, t)], axis=0)

    def chunk(c, _):
        rs = pl.ds(pl.multiple_of(c * C, C), C)
        ld_c = s_ld[rs, :]
        ld_hi, ld_mid, ld_lo = _split3(ld_c)
        cum = _dot(tri, ld_hi) + _dot(tri, ld_mid) + _dot(tri, ld_lo)
        cum_last = cum[C - 1:C]
        e_in = jnp.exp(cum)
        e_ex = jnp.exp(cum - ld_c)
        e_neg = jnp.exp(-cum)
        e_end = jnp.exp(cum_last - cum)
        g_end = jnp.exp(cum_last)
        at = s_an[rs, :] * e_ex
        rt = s_r[rs, :] * e_in
        bt = s_b[rs, :] * e_neg
        kt = s_k[rs, :] * e_neg
        bg = s_b[rs, :] * e_end
        kg = s_k[rs, :] * e_end
        vv = s_v[rs, :]
        pairs = range(4)
        lss = [slice(p * 128, (p + 1) * 128) for p in pairs]
        bf = lambda xs: [x.astype(BF16) for x in xs]
        ar_b = bf([jnp.concatenate([stack_masked(at[:, ls]), stack_masked(rt[:, ls])], axis=0) for ls in lss])
        bk_b = bf([jnp.concatenate([bt[:, ls], bt[:, ls], kt[:, ls], kt[:, ls]], axis=0) for ls in lss])
        vv_b = bf([stack_masked(vv[:, ls]) for ls in lss])
        bkg_b = bf([jnp.concatenate([stack_masked(bg[:, ls]), stack_masked(kg[:, ls])], axis=0) for ls in lss])
        big = [_dot_nt(a, b) for a, b in zip(ar_b, bk_b)]
        a_ab = [jnp.where(strict, m[0:2 * C, 0:2 * C], 0.0) for m in big]
        a_ak = bf([jnp.where(strict, m[0:2 * C, 2 * C:4 * C], 0.0) for m in big])
        a_rbk = bf([jnp.concatenate([jnp.where(incl, m[2 * C:4 * C, 0:2 * C], 0.0),
                                     jnp.where(incl, m[2 * C:4 * C, 2 * C:4 * C], 0.0)], axis=1) for m in big])
        inv = [eye + a for a in a_ab]
        pw = a_ab
        for _ in range(5):
            pw_b = bf(pw)
            pw = [_dot(x, x) for x in pw_b]
            inv = [iv + _dot(x.astype(BF16), iv.astype(BF16)) for x, iv in zip(pw, inv)]
        st = [s_ref[0, p] for p in pairs]
        from_state = [_dot_nt(a, s.astype(BF16)) for a, s in zip(ar_b, st)]
        rhs = [fs[0:2 * C] + _dot(a, v) for fs, a, v in zip(from_state, a_ak, vv_b)]
        u_b = bf([_dot(iv.astype(BF16), r.astype(BF16)) for iv, r in zip(inv, rhs)])
        uv_b = [jnp.concatenate([u, v], axis=0) for u, v in zip(u_b, vv_b)]
        y_s = [fs[2 * C:4 * C] + _dot(a, uv) for fs, a, uv in zip(from_state, a_rbk, uv_b)]
        for p in pairs:
            s_y[rs, lss[p]] = y_s[p][0:C] + y_s[p][C:2 * C]
            s_ref[0, p] = st[p] * g_end[:, lss[p]] + _dot_tn(uv_b[p], bkg_b[p])
        return 0

    lax.fori_loop(0, T // C, chunk, 0, unroll=True)

    y = s_y[...]
    mean = head_sums(y) * (1.0 / HEAD_DIM)
    d = y - mean
    var = head_sums(d * d) * (1.0 / HEAD_DIM)
    yn = d * lax.rsqrt(var + GN_EPS) * lnw_ref[...] + lnb_ref[...]
    bonus = head_sums(r * kmod * rk_ref[...]) * v
    ob_ref[0] = ((yn + bonus) * g).astype(BF16)


def _rwkv(rkv, lora, prev_rkv, prev_lora, s0_bd, wts):
    bsz, seq, _ = rkv.shape
    C = RWKV_CHUNK
    T = min(256, seq)
    head_of = jnp.arange(128, dtype=jnp.int32) // HEAD_DIM
    gmat = (head_of[:, None] == head_of[None, :]).astype(BF16)
    tri = (jnp.arange(C)[:, None] >= jnp.arange(C)[None, :]).astype(BF16)
    consts = list(wts) + [gmat, tri]
    seq_spec = lambda w: pl.BlockSpec((1, T, w), lambda b, i: (b, i, 0))
    per_b = lambda a: pl.BlockSpec((1,) + a.shape[1:], lambda b, i: (b,) + (0,) * (a.ndim - 1))
    full = lambda a: pl.BlockSpec(a.shape, lambda b, i: (0,) * a.ndim)
    big = lambda: pltpu.VMEM((T, R_WIDTH), F32)
    return pl.pallas_call(
        functools.partial(_rwkv_kernel, T=T, C=C),
        grid=(bsz, seq // T),
        in_specs=[seq_spec(RKV_W), seq_spec(LORA_PAD), per_b(prev_rkv), per_b(prev_lora), per_b(s0_bd)]
        + [full(a) for a in consts],
        out_specs=[seq_spec(R_WIDTH), per_b(s0_bd)],
        out_shape=[jax.ShapeDtypeStruct((bsz, seq, R_WIDTH), BF16), jax.ShapeDtypeStruct(s0_bd.shape, F32)],
        scratch_shapes=[pltpu.VMEM((1, RKV_W), F32), pltpu.VMEM((1, LORA_PAD), F32)] + [big() for _ in range(7)],
        compiler_params=_cparams(("parallel", "arbitrary")),
        name="rwkv",
    )(rkv, lora, prev_rkv, prev_lora, s0_bd, *consts)


def _rwkv_step_kernel(rkv_ref, lora_ref, prkv_ref, plora_ref, s0_ref, mu_rkv_ref, mu_lora_ref, w0_ref, wdec_ref,
                      a0_ref, wa_ref, wg_ref, kk_ref, ka_ref, rk_ref, lnw_ref, lnb_ref,
                      ob_ref, s_ref, p_r, p_w, p_k, p_v, p_kk, p_a, p_g):
    b = pl.program_id(0)

    @pl.when(b == 0)
    def _():
        x = rkv_ref[...]
        xl = lora_ref[...]
        pm = x + (prkv_ref[...] - x) * mu_rkv_ref[...]
        pml = xl + (plora_ref[...] - xl) * mu_lora_ref[...]
        k = pm[:, R_WIDTH:2 * R_WIDTH]
        t01 = pml[:, 0:128]
        dec_arg = w0_ref[...] + _mxu(jnp.tanh(t01), wdec_ref[...], True)
        sp = jnp.maximum(-dec_arg, 0.0) + jnp.log(1.0 + jnp.exp(-jnp.abs(dec_arg)))
        a = jax.nn.sigmoid(a0_ref[...] + _mxu(t01, wa_ref[...], True))
        p_r[...] = pm[:, 0:R_WIDTH]
        p_w[...] = jnp.exp(-jnp.exp(-sp - 0.5))
        p_k[...] = k * (1.0 + (a - 1.0) * ka_ref[...])
        p_v[...] = pm[:, 2 * R_WIDTH:3 * R_WIDTH]
        p_kk[...] = k * kk_ref[...]
        p_a[...] = a
        p_g[...] = _mxu(jax.nn.sigmoid(pml[:, 128:LORA_PAD]), wg_ref[...], True)
        ob_ref[...] = jnp.zeros(ob_ref.shape, F32)

    rows = pl.ds(pl.multiple_of((b // 8) * 8, 8), 8)
    mine = lax.broadcasted_iota(jnp.int32, (8, 128), 0) == b % 8
    ri = lax.broadcasted_iota(jnp.int32, (128, 128), 0)
    ci = lax.broadcasted_iota(jnp.int32, (128, 128), 1)
    same_head = (ri >= HEAD_DIM) == (ci >= HEAD_DIM)
    eye = jnp.where(ri == ci, 1.0, 0.0)
    top = lax.broadcasted_iota(jnp.int32, (128, 1), 0) < HEAD_DIM
    left = lax.broadcasted_iota(jnp.int32, (1, 128), 1) < HEAD_DIM

    def per_head_rows(t):
        s0 = jnp.sum(jnp.where(left, t, 0.0), axis=-1, keepdims=True)
        s1 = jnp.sum(jnp.where(left, 0.0, t), axis=-1, keepdims=True)
        return jnp.where(left, s0, s1)

    def per_head_cols(t):
        s0 = jnp.sum(jnp.where(top, t, 0.0), axis=0, keepdims=True)
        s1 = jnp.sum(jnp.where(top, 0.0, t), axis=0, keepdims=True)
        return jnp.where(top, s0, s1)

    for p in range(4):
        ls = slice(p * 128, (p + 1) * 128)
        r, w, k, v, kk0, a, g = (jnp.sum(jnp.where(mine, t[rows, ls], 0.0), axis=0, keepdims=True)
                                 for t in (p_r, p_w, p_k, p_v, p_kk, p_a, p_g))
        kk = kk0 * lax.rsqrt(jnp.maximum(per_head_rows(kk0 * kk0), 1e-24))
        st = s0_ref[0, p]
        sa = jnp.sum(st * kk, axis=-1, keepdims=True)
        v_col = jnp.sum(eye * v, axis=-1, keepdims=True)
        st = st * w + jnp.where(same_head, v_col * k - sa * (kk * a), 0.0)
        s_ref[0, p] = st
        y = jnp.sum(st * r, axis=-1, keepdims=True)
        d = y - per_head_cols(y) * (1.0 / HEAD_DIM)
        yn = d * lax.rsqrt(per_head_cols(d * d) * (1.0 / HEAD_DIM) + GN_EPS)
        yn_row = jnp.sum(eye * yn, axis=0, keepdims=True)
        bonus = per_head_rows(r * k * rk_ref[:, ls]) * v
        out = (yn_row * lnw_ref[:, ls] + lnb_ref[:, ls] + bonus) * g
        ob_ref[rows, ls] = jnp.where(mine, out, ob_ref[rows, ls])


def _rwkv_step(rkv, lora, prev_rkv, prev_lora, s0_bd, wts):
    bs = rkv.shape[0]
    full = lambda a: pl.BlockSpec(a.shape, lambda b: (0,) * a.ndim)
    state = pl.BlockSpec((1,) + s0_bd.shape[1:], lambda b: (b, 0, 0, 0))
    args = [rkv, lora, prev_rkv, prev_lora]
    return pl.pallas_call(
        _rwkv_step_kernel,
        grid=(bs,),
        in_specs=[full(a) for a in args] + [state] + [full(a) for a in wts],
        out_specs=[pl.BlockSpec((bs, R_WIDTH), lambda b: (0, 0)), state],
        out_shape=[jax.ShapeDtypeStruct((bs, R_WIDTH), F32), jax.ShapeDtypeStruct(s0_bd.shape, F32)],
        scratch_shapes=[pltpu.VMEM((bs, R_WIDTH), F32) for _ in range(7)],
        compiler_params=_cparams(("arbitrary",)),
        name="rwkv_step",
    )(*args, s0_bd, *wts)


def _state_to_bd(s):
    b = s.shape[0]
    s = s.reshape(b, 4, 2, HEAD_DIM, HEAD_DIM)
    z = jnp.zeros_like(s[:, :, 0])
    top = jnp.concatenate([s[:, :, 0], z], axis=-1)
    bot = jnp.concatenate([z, s[:, :, 1]], axis=-1)
    return jnp.concatenate([top, bot], axis=-2)


def _state_from_bd(s):
    b = s.shape[0]
    h0 = s[:, :, :HEAD_DIM, :HEAD_DIM]
    h1 = s[:, :, HEAD_DIM:, HEAD_DIM:]
    return jnp.stack([h0, h1], axis=2).reshape(b, 8, HEAD_DIM, HEAD_DIM)


def _merge_kernel(x_ref, gain_ref, wg_ref, oa_ref, ob_ref, wa_ref, wb_ref, wo_ref, o_ref, *, precise):
    x = x_ref[...]
    d = x.shape[1]
    gates = jax.nn.sigmoid(_mxu(_rms(x, gain_ref[...]), wg_ref[...], precise))
    ya = _mxu(oa_ref[...], wa_ref[...], precise)
    yb = _mxu(ob_ref[...], wb_ref[...], precise)
    merged = gates[:, :d] * ya + gates[:, d:] * yb
    o_ref[...] = x + _mxu(merged, wo_ref[...], precise)


def _merge(x, gain, wg, oa, ob, wa, wb, wo, precise=False):
    n, d = x.shape
    tm = min(512, n)
    row = lambda w: pl.BlockSpec((tm, w), lambda i: (i, 0))
    full = lambda a: pl.BlockSpec(a.shape, lambda i: (0, 0))
    return pl.pallas_call(
        functools.partial(_merge_kernel, precise=precise),
        grid=(n // tm,),
        in_specs=[row(d), full(gain), full(wg), row(A_WIDTH), row(R_WIDTH), full(wa), full(wb), full(wo)],
        out_specs=row(d),
        out_shape=jax.ShapeDtypeStruct((n, d), F32),
        compiler_params=_cparams(("parallel",)),
        name="merge",
    )(x, gain, wg, oa, ob, wa, wb, wo)


def _ffn_kernel(x_ref, gain_ref, w1_ref, w3_ref, w2_ref, o_ref, h_scr, acc, *, precise):
    f = pl.program_id(1)

    @pl.when(f == 0)
    def _():
        h_scr[...] = _rms(x_ref[...], gain_ref[...]).astype(h_scr.dtype)
        acc[...] = x_ref[...]

    h = h_scr[...]
    t = jax.nn.silu(_mxu(h, w1_ref[...], precise)) * _mxu(h, w3_ref[...], precise)
    acc[...] += _mxu(t, w2_ref[...], precise)

    @pl.when(f == pl.num_programs(1) - 1)
    def _():
        o_ref[...] = acc[...]


def _ffn(x, gain, w1, w3, w2, precise=False):
    n, d = x.shape
    dff = w1.shape[1]
    tm = min(512, n)
    tf = dff // 2
    return pl.pallas_call(
        functools.partial(_ffn_kernel, precise=precise),
        grid=(n // tm, dff // tf),
        in_specs=[pl.BlockSpec((tm, d), lambda i, f: (i, 0)), pl.BlockSpec((1, d), lambda i, f: (0, 0)),
                  pl.BlockSpec((d, tf), lambda i, f: (0, f)), pl.BlockSpec((d, tf), lambda i, f: (0, f)),
                  pl.BlockSpec((tf, d), lambda i, f: (f, 0))],
        out_specs=pl.BlockSpec((tm, d), lambda i, f: (i, 0)),
        out_shape=jax.ShapeDtypeStruct((n, d), F32),
        scratch_shapes=[pltpu.VMEM((tm, d), F32 if precise else BF16), pltpu.VMEM((tm, d), F32)],
        compiler_params=_cparams(("parallel", "arbitrary")),
        name="ffn",
    )(x, gain, w1, w3, w2)


def _router_kernel(x_ref, gain_ref, wr_ref, o_ref, *, n_exp):
    logits = _mxu(_rms(x_ref[...], gain_ref[...]), wr_ref[...], True)
    lane = lax.broadcasted_iota(jnp.int32, logits.shape, 1)
    lanef = lane.astype(F32)
    logits = jnp.where(lane < n_exp, logits, -jnp.inf)
    m1 = jnp.max(logits, axis=-1, keepdims=True)
    e1 = jnp.min(jnp.where(logits == m1, lanef, 1e9), axis=-1, keepdims=True)
    rest = jnp.where(lanef == e1, -jnp.inf, logits)
    m2 = jnp.max(rest, axis=-1, keepdims=True)
    e2 = jnp.min(jnp.where(rest == m2, lanef, 1e9), axis=-1, keepdims=True)
    z = jnp.exp(m2 - m1)
    g1 = 1.0 / (1.0 + z)
    g2 = z / (1.0 + z)
    res = jnp.where(lane == 0, e1, jnp.where(lane == 1, e2, jnp.where(lane == 2, g1, jnp.where(lane == 3, g2, 0.0))))
    o_ref[...] = res[:, 0:8]


def _router(x, gain, w_router):
    n, d = x.shape
    n_exp = w_router.shape[1]
    tm = min(512, n)
    wp = jnp.zeros((d, 128), F32).at[:, :n_exp].set(w_router)
    return pl.pallas_call(
        functools.partial(_router_kernel, n_exp=n_exp),
        grid=(n // tm,),
        in_specs=[pl.BlockSpec((tm, d), lambda i: (i, 0)), pl.BlockSpec((1, d), lambda i: (0, 0)),
                  pl.BlockSpec((d, 128), lambda i: (0, 0))],
        out_specs=pl.BlockSpec((tm, 8), lambda i: (i, 0)),
        out_shape=jax.ShapeDtypeStruct((n, 8), F32),
        compiler_params=_cparams(("parallel",)),
        name="router",
    )(x, gain, wp)


def _experts_kernel(be_ref, xb_ref, gain_ref, w1_ref, w3_ref, w2_ref, o_ref, *wb, precise):
    h = _rms(xb_ref[...], gain_ref[...])
    if precise:
        t = jax.nn.silu(_mxu(h, w1_ref[0], True)) * _mxu(h, w3_ref[0], True)
        o_ref[...] = _mxu(t, w2_ref[0], True)
        return
    i = pl.program_id(0)
    w1b, w3b, w2b = wb

    @pl.when((i == 0) | (be_ref[i] != be_ref[jnp.maximum(i - 1, 0)]))
    def _():
        w1b[...] = w1_ref[0].astype(BF16)
        w3b[...] = w3_ref[0].astype(BF16)
        w2b[...] = w2_ref[0].astype(BF16)

    h = h.astype(BF16)
    t = jax.nn.silu(_dot(h, w1b[...])) * _dot(h, w3b[...])
    o_ref[...] = _dot(t.astype(BF16), w2b[...])


def _experts(blk_e, xb, gain, w1, w3, w2, rb, precise):
    r, d = xb.shape
    dffe = w1.shape[2]
    wspec = lambda shape: pl.BlockSpec((1,) + shape, lambda i, be: (be[i], 0, 0))
    scratch = [] if precise else [pltpu.VMEM((d, dffe), BF16), pltpu.VMEM((d, dffe), BF16), pltpu.VMEM((dffe, d), BF16)]
    return pl.pallas_call(
        functools.partial(_experts_kernel, precise=precise),
        grid_spec=pltpu.PrefetchScalarGridSpec(
            num_scalar_prefetch=1,
            grid=(r // rb,),
            in_specs=[pl.BlockSpec((rb, d), lambda i, be: (i, 0)), pl.BlockSpec((1, d), lambda i, be: (0, 0)),
                      wspec((d, dffe)), wspec((d, dffe)), wspec((dffe, d))],
            out_specs=pl.BlockSpec((rb, d), lambda i, be: (i, 0)),
            scratch_shapes=scratch,
        ),
        out_shape=jax.ShapeDtypeStruct((r, d), F32),
        compiler_params=_cparams(("arbitrary",)),
        name="experts",
    )(blk_e, xb, gain, w1, w3, w2)


def _combine_kernel(x_ref, y0_ref, y1_ref, route_ref, *rest, final):
    g = route_ref[...]
    out = x_ref[...] + g[:, MOE_TOPK:MOE_TOPK + 1] * y0_ref[...] + g[:, MOE_TOPK + 1:MOE_TOPK + 2] * y1_ref[...]
    if final:
        gain_ref, o_ref = rest
        o_ref[...] = _rms(out, gain_ref[...])
    else:
        rest[0][...] = out


def _combine(x, y0, y1, route, final_gain):
    n, d = x.shape
    tm = min(512, n)
    row = pl.BlockSpec((tm, d), lambda i: (i, 0))
    extra = [] if final_gain is None else [final_gain]
    return pl.pallas_call(
        functools.partial(_combine_kernel, final=final_gain is not None),
        grid=(n // tm,),
        in_specs=[row, row, row, pl.BlockSpec((tm, 8), lambda i: (i, 0))] + [pl.BlockSpec((1, d), lambda i: (0, 0))] * len(extra),
        out_specs=row,
        out_shape=jax.ShapeDtypeStruct((n, d), F32),
        compiler_params=_cparams(("parallel",)),
        name="moe_combine",
    )(x, y0, y1, route, *extra)


def _moe(x, gain, w_router, w1, w3, w2, rb, precise=False, final_gain=None, anchor=None):
    n, d = x.shape
    n_exp = w_router.shape[1]
    route = _router(x, gain, w_router)
    top_e = route[:, 0:MOE_TOPK].astype(jnp.int32)
    na = n * MOE_TOPK
    e_flat = top_e.reshape(na)
    onehot = (e_flat[:, None] == jnp.arange(n_exp, dtype=jnp.int32)[None, :]).astype(jnp.int32)
    within = jnp.cumsum(onehot, axis=0) - onehot
    counts = jnp.sum(onehot, axis=0)
    padded = (counts + rb - 1) // rb * rb
    pend = jnp.cumsum(padded)
    pstart = pend - padded
    dest = pstart[e_flat] + jnp.sum(within * onehot, axis=1)
    r = (na + n_exp * (rb - 1) + rb - 1) // rb * rb
    tok = jnp.arange(na, dtype=jnp.int32) // MOE_TOPK
    row_tok = jnp.zeros((r,), jnp.int32).at[dest].set(tok, unique_indices=True)
    blk_start = jnp.arange(r // rb, dtype=jnp.int32) * rb
    blk_e = jnp.minimum(jnp.searchsorted(pend, blk_start, side="right"), n_exp - 1).astype(jnp.int32)
    xb = x[row_tok]
    yield
    if anchor is not None and anchor:
        blk_e, _ = lax.optimization_barrier((blk_e, anchor[-1]))
    yb = _experts(blk_e, xb, gain, w1, w3, w2, rb, precise)
    pos = dest.reshape(n, MOE_TOPK)
    y0, y1 = yb[pos[:, 0]], yb[pos[:, 1]]
    yield
    return _combine(x, y0, y1, route, final_gain)


def _norm_kernel(x_ref, gain_ref, o_ref):
    o_ref[...] = _rms(x_ref[...], gain_ref[...])


def _final_norm(x, gain):
    n, d = x.shape
    tm = min(1024, n)
    return pl.pallas_call(
        _norm_kernel,
        grid=(n // tm,),
        in_specs=[pl.BlockSpec((tm, d), lambda i: (i, 0)), pl.BlockSpec((1, d), lambda i: (0, 0))],
        out_specs=pl.BlockSpec((tm, d), lambda i: (i, 0)),
        out_shape=jax.ShapeDtypeStruct((n, d), F32),
        compiler_params=_cparams(("parallel",)),
        name="final_norm",
    )(x, gain)


MIXER_WEIGHTS = ("norm_attn", "w_in", "w_up_a", "mu_shift", "w0", "w_dec2", "a0", "w_a2", "w_g2", "k_k", "k_a", "r_k",
                 "ln_x_w", "ln_x_b", "w_up_b", "w_out")


def _layer_weights(l, dt, norm_attn, w_in, w_up_a, mu_shift, w0, w_dec2, a0, w_a2, w_g2, k_k, k_a, r_k, ln_x_w, ln_x_b,
                   w_up_b, w_out):
    d = w_in.shape[1]
    wi = w_in[l]
    c0 = 3 * A_WIDTH
    c1 = c0 + RKV_W
    n_lora = LORA_DECAY + LORA_ICLR + LORA_GATE
    c2 = c1 + n_lora
    wqkv = wi[:, :c0].astype(dt)
    wrkv = wi[:, c0:c1].astype(dt)
    wlora = jnp.zeros((d, LORA_PAD), dt).at[:, :n_lora].set(wi[:, c1:c2].astype(dt))
    wgate = wi[:, c2:].astype(dt)
    mu = mu_shift[l]
    mu_rkv = mu[:RKV_W][None]
    mu_lora = jnp.zeros((1, LORA_PAD), F32).at[0, :n_lora].set(mu[RKV_W:])
    wdec = jnp.zeros((128, R_WIDTH), dt).at[:LORA_DECAY].set(w_dec2[l].astype(dt))
    wa = jnp.zeros((128, R_WIDTH), dt).at[LORA_DECAY:LORA_DECAY + LORA_ICLR].set(w_a2[l].astype(dt))
    wg = jnp.zeros((LORA_PAD - 128, R_WIDTH), dt).at[:LORA_GATE].set(w_g2[l].astype(dt))
    rw = (mu_rkv, mu_lora, w0[l][None], wdec, a0[l][None], wa, wg, k_k[l][None], k_a[l][None],
          r_k[l].reshape(1, R_WIDTH), ln_x_w[l][None], ln_x_b[l][None])
    return dict(gain=norm_attn[l][None], wqkv=wqkv, wrkv=wrkv, wlora=wlora, wgate=wgate, rw=rw,
                wa=w_up_a[l].astype(dt), wb=w_up_b[l].astype(dt), wo=w_out[l].astype(dt), n_lora=n_lora)


def _mixer_prompt(x, bsz, seq, w):
    qt, kb, va, k, v, rkv, lora = _proj(x, w["gain"], w["wqkv"], w["wrkv"], w["wlora"], seq=seq)
    oa = _moba_prompt(qt, kb, va, _kmean(k), bsz, seq)
    zeros = lambda width: jnp.zeros((bsz, 1, width), F32)
    ob, s_bd = _rwkv(rkv.reshape(bsz, seq, RKV_W), lora.reshape(bsz, seq, LORA_PAD), zeros(RKV_W), zeros(LORA_PAD),
                     jnp.zeros((bsz, 4, 128, 128), F32), w["rw"])
    x = _merge(x, w["gain"], w["wgate"], oa, ob.reshape(bsz * seq, R_WIDTH), w["wa"], w["wb"], w["wo"])
    last = lambda t: t.reshape(bsz, seq, -1)[:, -1]
    p_last = jnp.concatenate([last(rkv), last(lora)[:, :w["n_lora"]]], axis=-1)
    return x, k, v, _state_from_bd(s_bd), p_last


def _mixer_sample(x, page_table, cache_kt, cache_vt, layer, s0, p_prev, w):
    bs = x.shape[0]
    q, k, v, rkv, lora = _proj(x, w["gain"], w["wqkv"], w["wrkv"], w["wlora"], precise=True)
    kmean_t = _kmean_paged(page_table, cache_kt, layer)
    sel = _select(q, kmean_t)[:, :, :MOBA_TOPK].reshape(bs, A_HEADS * MOBA_TOPK)
    oa = _decode_attn(page_table, sel, q, k, v, cache_kt, cache_vt, layer)
    prev_lora = jnp.zeros((bs, LORA_PAD), F32).at[:, :w["n_lora"]].set(p_prev[:, RKV_W:])
    ob, s_bd = _rwkv_step(rkv, lora, p_prev[:, :RKV_W], prev_lora, _state_to_bd(s0), w["rw"])
    x = _merge(x, w["gain"], w["wgate"], oa, ob, w["wa"], w["wb"], w["wo"], precise=True)
    p_last = jnp.concatenate([rkv, lora[:, :w["n_lora"]]], axis=-1)
    return x, k, v, _state_from_bd(s_bd), p_last


def _closing_gain(wts, l):
    last = l == wts["w_in"].shape[0] - 1
    return wts["norm_final"][None] if last and l % 2 == 1 else None


def _prompt_group(x_prompt, wts, other_group):
    bp, seq, d = x_prompt.shape
    x = x_prompt.reshape(bp * seq, d)
    outs = [[] for _ in range(4)]
    for l in range(wts["w_in"].shape[0]):
        w = _layer_weights(l, BF16, *(wts[n] for n in MIXER_WEIGHTS))
        x, k, v, s, sh = _mixer_prompt(x, bp, seq, w)
        for o, t in zip(outs, (k.reshape(bp, seq, A_HEADS, HEAD_DIM), v.reshape(bp, seq, A_HEADS, HEAD_DIM), s, sh)):
            o.append(t)
        gain = wts["norm_ffn"][l][None]
        i = l // 2
        bf = lambda t: t[i].astype(BF16)
        if l % 2 == 0:
            yield
            x = _ffn(x, gain, bf(wts["ffn_w1"]), bf(wts["ffn_w3"]), bf(wts["ffn_w2"]))
            yield
        else:
            x = yield from _moe(x, gain, wts["router"][i], wts["moe_w1"][i], wts["moe_w3"][i], wts["moe_w2"][i], 256,
                                final_gain=_closing_gain(wts, l), anchor=other_group)
    y = x if _closing_gain(wts, l) is not None else _final_norm(x, wts["norm_final"][None])
    return (y.reshape(bp, seq, d),) + tuple(jnp.stack(o) for o in outs)


def _sample_group(x_sample, cache_k, cache_v, state_wkv, state_shift, page_table, wts, mixed):
    bs, dec_seq, d = x_sample.shape
    depth, n_pool, page = cache_k.shape[:3]
    assert dec_seq == 1 and 2 * page == MOBA_BLOCK
    cache_kt = cache_k.transpose(0, 1, 3, 4, 2).reshape(depth, n_pool, A_WIDTH, page)
    cache_vt = cache_v.transpose(0, 1, 3, 4, 2).reshape(depth, n_pool, A_WIDTH, page)
    x = x_sample.reshape(bs, d)
    outs = [[] for _ in range(4)]
    for l in range(depth):
        w = _layer_weights(l, F32, *(wts[n] for n in MIXER_WEIGHTS))
        x, k, v, s, sh = _mixer_sample(x, page_table, cache_kt, cache_vt, l, state_wkv[l], state_shift[l], w)
        mixed.append(x)
        for o, t in zip(outs, (k.reshape(bs, 1, A_HEADS, HEAD_DIM), v.reshape(bs, 1, A_HEADS, HEAD_DIM), s, sh)):
            o.append(t)
        gain = wts["norm_ffn"][l][None]
        i = l // 2
        if l % 2 == 0:
            yield
            x = _ffn(x, gain, wts["ffn_w1"][i], wts["ffn_w3"][i], wts["ffn_w2"][i], precise=True)
            yield
        else:
            x = yield from _moe(x, gain, wts["router"][i], wts["moe_w1"][i], wts["moe_w3"][i], wts["moe_w2"][i], 32,
                                precise=True, final_gain=_closing_gain(wts, l))
    y = x if _closing_gain(wts, l) is not None else _final_norm(x, wts["norm_final"][None])
    return (y.reshape(bs, 1, d),) + tuple(jnp.stack(o) for o in outs)


def _alternate(*stages):
    results = [None] * len(stages)
    live = list(enumerate(stages))
    while live:
        for item in list(live):
            try:
                next(item[1])
            except StopIteration as done:
                results[item[0]] = done.value
                live.remove(item)
    return results


def kernel(x_prompt, x_sample, cache_k, cache_v, state_wkv, state_shift, page_table, norm_attn, w_in, w_up_a, mu_shift, w0, w_dec2, a0, w_a2, w_g2, k_k, k_a, r_k, ln_x_w, ln_x_b, w_up_b, w_out, norm_ffn, ffn_w1, ffn_w3, ffn_w2, router, moe_w1, moe_w3, moe_w2, norm_final):
    assert x_prompt.shape[1] % MOBA_BLOCK == 0
    wts = dict(norm_attn=norm_attn, w_in=w_in, w_up_a=w_up_a, mu_shift=mu_shift, w0=w0, w_dec2=w_dec2, a0=a0, w_a2=w_a2,
               w_g2=w_g2, k_k=k_k, k_a=k_a, r_k=r_k, ln_x_w=ln_x_w, ln_x_b=ln_x_b, w_up_b=w_up_b, w_out=w_out,
               norm_ffn=norm_ffn, ffn_w1=ffn_w1, ffn_w3=ffn_w3, ffn_w2=ffn_w2, router=router, moe_w1=moe_w1,
               moe_w3=moe_w3, moe_w2=moe_w2, norm_final=norm_final)
    sample_mixed = []
    (y_p, k_p, v_p, s_p, sh_p), (y_s, k_s, v_s, s_s, sh_s) = _alternate(
        _prompt_group(x_prompt, wts, sample_mixed),
        _sample_group(x_sample, cache_k, cache_v, state_wkv, state_shift, page_table, wts, sample_mixed))
    return (y_p, y_s, k_p, v_p, s_p, sh_p, k_s, v_s, s_s, sh_s)
```

```python
import functools

import jax
import jax.numpy as jnp
from jax import lax
from jax.experimental import pallas as pl
from jax.experimental.pallas import tpu as pltpu

F32 = jnp.float32
BF16 = jnp.bfloat16

A_HEADS = 8
HEAD_DIM = 64
A_WIDTH = 512
R_WIDTH = 512
MOBA_BLOCK = 256
MOBA_TOPK = 3
LORA_DECAY = 64
LORA_ICLR = 64
LORA_GATE = 160
LORA_PAD = 384
RKV_W = 3 * R_WIDTH
GN_EPS = 64e-5
RMS_EPS = 1e-6
MOE_TOPK = 2
RWKV_CHUNK = 64
NEG = -1e30
Q_SCALE = HEAD_DIM ** -0.5 * 1.4426950408889634
V_AUG = HEAD_DIM + 16
VMEM_LIMIT = 56 * 1024 * 1024


def _dot(a, b):
    return jnp.dot(a, b, preferred_element_type=F32)


def _dot_nt(a, b):
    return lax.dot_general(a, b, (((1,), (1,)), ((), ())), preferred_element_type=F32)


def _dot_tn(a, b):
    return lax.dot_general(a, b, (((0,), (0,)), ((), ())), preferred_element_type=F32)


def _mxu(a, b, precise):
    if precise:
        return jnp.dot(a.astype(F32), b.astype(F32), precision=lax.Precision.HIGHEST, preferred_element_type=F32)
    return jnp.dot(a.astype(BF16), b.astype(BF16), preferred_element_type=F32)


def _split2(x):
    hi = x.astype(BF16)
    lo = (x - hi.astype(F32)).astype(BF16)
    return hi, lo


def _split3(x):
    hi = x.astype(BF16)
    r1 = x - hi.astype(F32)
    mid = r1.astype(BF16)
    lo = (r1 - mid.astype(F32)).astype(BF16)
    return hi, mid, lo


def _dot_x2(x, m):
    hi, lo = _split2(x)
    return _dot(hi, m) + _dot(lo, m)


def _rms(x, gain):
    return x * lax.rsqrt(jnp.mean(x * x, axis=-1, keepdims=True) + RMS_EPS) * gain


def _cparams(sem):
    return pltpu.CompilerParams(dimension_semantics=sem, vmem_limit_bytes=VMEM_LIMIT)


def _proj_kernel(x_ref, gain_ref, wqkv_ref, wrkv_ref, wlora_ref, q_ref, *outs, precise):
    k_ref, v_ref, rkv_ref, lora_ref = outs[-4:]
    h = _rms(x_ref[...], gain_ref[...])
    h = h if precise else h.astype(BF16)
    q = _mxu(h, wqkv_ref[:, 0:A_WIDTH], precise) * Q_SCALE
    k = _mxu(h, wqkv_ref[:, A_WIDTH:2 * A_WIDTH], precise)
    v = _mxu(h, wqkv_ref[:, 2 * A_WIDTH:3 * A_WIDTH], precise)
    k_ref[...] = k
    v_ref[...] = v
    rkv_ref[...] = _mxu(h, wrkv_ref[...], precise)
    lora_ref[...] = _mxu(h, wlora_ref[...], precise)
    if precise:
        q_ref[...] = q
        return
    kb_ref, vb_ref = outs[:2]
    q_ref[0] = q.T.astype(BF16)
    kb_ref[...] = k.astype(BF16)
    ones = jnp.ones((V_AUG - HEAD_DIM, MOBA_BLOCK), F32)
    for u in range(v.shape[0] // MOBA_BLOCK):
        vt = v[u * MOBA_BLOCK:(u + 1) * MOBA_BLOCK].T
        parts = []
        for hd in range(A_HEADS):
            parts += [vt[hd * HEAD_DIM:(hd + 1) * HEAD_DIM], ones]
        vb_ref[0, u] = jnp.concatenate(parts, axis=0).astype(BF16)


def _proj(x, gain, wqkv, wrkv, wlora, seq=None, precise=False):
    n, d = x.shape
    tm = min(512, n)
    row = lambda w: pl.BlockSpec((tm, w), lambda i: (i, 0))
    full = lambda a: pl.BlockSpec(a.shape, lambda i: (0, 0))
    if precise:
        attn_specs = [row(A_WIDTH)]
        attn_shapes = [jax.ShapeDtypeStruct((n, A_WIDTH), F32)]
    else:
        tps = seq // tm
        nbt = tm // MOBA_BLOCK
        attn_specs = [pl.BlockSpec((1, A_WIDTH, tm), lambda i: (i // tps, 0, i % tps)), row(A_WIDTH),
                      pl.BlockSpec((1, nbt, A_HEADS * V_AUG, MOBA_BLOCK), lambda i: (i // tps, i % tps, 0, 0))]
        attn_shapes = [jax.ShapeDtypeStruct((n // seq, A_WIDTH, seq), BF16), jax.ShapeDtypeStruct((n, A_WIDTH), BF16),
                       jax.ShapeDtypeStruct((n // seq, seq // MOBA_BLOCK, A_HEADS * V_AUG, MOBA_BLOCK), BF16)]
    return pl.pallas_call(
        functools.partial(_proj_kernel, precise=precise),
        grid=(n // tm,),
        in_specs=[row(d), full(gain), full(wqkv), full(wrkv), full(wlora)],
        out_specs=attn_specs + [row(A_WIDTH)] * 2 + [row(RKV_W), row(LORA_PAD)],
        out_shape=attn_shapes + [jax.ShapeDtypeStruct((n, A_WIDTH), F32)] * 2
        + [jax.ShapeDtypeStruct((n, RKV_W), F32), jax.ShapeDtypeStruct((n, LORA_PAD), F32)],
        compiler_params=_cparams(("parallel",)),
        name="proj",
    )(x, gain, wqkv, wrkv, wlora)


def _kmean_kernel(k_ref, o_ref):
    g = o_ref.shape[0]
    x = k_ref[...].reshape(g, MOBA_BLOCK, A_WIDTH)
    o_ref[...] = jnp.sum(x, axis=1) * (1.0 / MOBA_BLOCK)


def _kmean(k):
    n = k.shape[0]
    nblk = n // MOBA_BLOCK
    g = min(8, nblk)
    return pl.pallas_call(
        _kmean_kernel,
        grid=(nblk // g,),
        in_specs=[pl.BlockSpec((g * MOBA_BLOCK, A_WIDTH), lambda i: (i, 0))],
        out_specs=pl.BlockSpec((g, A_WIDTH), lambda i: (i, 0)),
        out_shape=jax.ShapeDtypeStruct((nblk, A_WIDTH), F32),
        compiler_params=_cparams(("parallel",)),
        name="kmean",
    )(k)


def _moba_kernel(qt_ref, k_ref, va_ref, km_ref, o_ref, bias_ref, s_scr, *, nb, unroll):
    blk = MOBA_BLOCK
    tq = 2 * blk
    b0 = 2 * pl.program_id(2)
    qt = qt_ref[0]
    row = lax.broadcasted_iota(jnp.int32, qt.shape, 0)
    km_hi, km_lo = _split2(km_ref[0])
    bid = lax.broadcasted_iota(jnp.int32, (nb, tq), 0)
    past = bid < b0 + (lax.broadcasted_iota(jnp.int32, (nb, tq), 1) >= blk).astype(jnp.int32)
    qms = []
    b0_rows = []
    for hh in range(2):
        qm = jnp.where((row >= hh * HEAD_DIM) & (row < (hh + 1) * HEAD_DIM), qt, jnp.zeros_like(qt))
        qms.append(qm)
        gate = _dot(km_hi, qm) + _dot(km_lo, qm)
        gate = jnp.where(past, gate, -jnp.inf)
        bias = jnp.full((nb, tq), NEG, F32)
        for _ in range(MOBA_TOPK):
            best = jnp.max(gate, axis=0, keepdims=True)
            first = jnp.min(jnp.where(gate == best, bid, nb), axis=0, keepdims=True)
            pick = bid == first
            bias = jnp.where(pick & past, 0.0, bias)
            gate = jnp.where(pick, -jnp.inf, gate)
        b0_rows.append(jnp.sum(jnp.where(bid == b0, bias, 0.0), axis=0, keepdims=True))
        bias_ref[hh] = jnp.where(bid < b0, bias, NEG)

    kio = lax.broadcasted_iota(jnp.int32, (blk, tq), 0)
    qio = lax.broadcasted_iota(jnp.int32, (blk, tq), 1)
    lower = qio < blk
    mask_b1 = jnp.where(~lower & (kio <= qio - blk), 0.0, NEG)
    ms = []
    for hh in range(2):
        mask_b0 = jnp.where(lower, jnp.where(kio <= qio, 0.0, NEG), b0_rows[hh])
        s0 = _dot(k_ref[0, b0], qms[hh]) + mask_b0
        s1 = _dot(k_ref[0, b0 + 1], qms[hh]) + mask_b1
        s_scr[hh, nb] = s0
        s_scr[hh, nb + 1] = s1
        ms.append(jnp.maximum(jnp.max(s0, axis=0, keepdims=True), jnp.max(s1, axis=0, keepdims=True)))
    n_trips = (b0 + unroll - 1) // unroll

    def scores(jj, ms):
        ms = list(ms)
        for u in range(unroll):
            j = unroll * jj + u
            kj = k_ref[0, j]
            for hh in range(2):
                s = _dot(kj, qms[hh]) + bias_ref[hh, pl.ds(j, 1), :]
                s_scr[hh, j] = s
                ms[hh] = jnp.maximum(ms[hh], jnp.max(s, axis=0, keepdims=True))
        return tuple(ms)

    ms = lax.fori_loop(0, n_trips, scores, tuple(ms))

    accs = []
    for hh in range(2):
        acc = None
        for d in range(2):
            p = jnp.exp2(s_scr[hh, nb + d] - ms[hh]).astype(BF16)
            part = _dot(va_ref[0, b0 + d][hh * V_AUG:(hh + 1) * V_AUG, :], p)
            acc = part if acc is None else acc + part
        accs.append(acc)

    def weighted(jj, accs):
        accs = list(accs)
        for u in range(unroll):
            j = unroll * jj + u
            vj = va_ref[0, j]
            for hh in range(2):
                p = jnp.exp2(s_scr[hh, j] - ms[hh]).astype(BF16)
                accs[hh] = accs[hh] + _dot(vj[hh * V_AUG:(hh + 1) * V_AUG, :], p)
        return tuple(accs)

    accs = lax.fori_loop(0, n_trips, weighted, tuple(accs))
    o = jnp.concatenate([a[0:HEAD_DIM] / a[HEAD_DIM:HEAD_DIM + 1] for a in accs], axis=0)
    o_ref[0] = o.T.astype(BF16)


def _moba_prompt(qt, kb, va, kmean, bsz, seq):
    nb = seq // MOBA_BLOCK
    assert nb % 2 == 0
    tq = 2 * MOBA_BLOCK
    k4 = kb.reshape(bsz, nb, MOBA_BLOCK, A_WIDTH)
    km = kmean.reshape(bsz, nb, A_WIDTH)
    out = pl.pallas_call(
        functools.partial(_moba_kernel, nb=nb, unroll=4 if nb % 4 == 0 else 2),
        grid=(bsz, A_HEADS // 2, nb // 2),
        in_specs=[
            pl.BlockSpec((1, 128, tq), lambda b, h, i: (b, h, i)),
            pl.BlockSpec((1, nb, MOBA_BLOCK, 128), lambda b, h, i: (b, 0, 0, h)),
            pl.BlockSpec((1, nb, 2 * V_AUG, MOBA_BLOCK), lambda b, h, i: (b, 0, h, 0)),
            pl.BlockSpec((1, nb, 128), lambda b, h, i: (b, 0, h)),
        ],
        out_specs=pl.BlockSpec((1, tq, 128), lambda b, h, i: (b, i, h)),
        out_shape=jax.ShapeDtypeStruct((bsz, seq, A_WIDTH), BF16),
        scratch_shapes=[pltpu.VMEM((2, nb, tq), F32), pltpu.VMEM((2, nb + 2, MOBA_BLOCK, tq), F32)],
        compiler_params=_cparams(("parallel", "parallel", "arbitrary")),
        name="moba_prompt",
    )(qt, k4, va, km)
    return out.reshape(bsz * seq, A_WIDTH)


def _kmean_paged_kernel(pt_ref, ck_ref, o_ref, buf, sem, *, layer, n_pages):
    b = pl.program_id(0)
    slot = b % 2

    def fetch(seq, to_slot):
        for p in range(n_pages):
            pltpu.make_async_copy(ck_ref.at[layer, pt_ref[seq, p]], buf.at[to_slot, p], sem.at[to_slot, p]).start()

    @pl.when(b == 0)
    def _():
        fetch(0, 0)

    @pl.when(b + 1 < pl.num_programs(0))
    def _():
        fetch(b + 1, 1 - slot)

    nb = n_pages // 2
    lane = lax.broadcasted_iota(jnp.int32, (A_WIDTH, nb), 1)
    res = jnp.zeros((A_WIDTH, nb), F32)
    for n in range(nb):
        for e in range(2):
            pltpu.make_async_copy(ck_ref.at[layer, 0], buf.at[slot, 2 * n + e], sem.at[slot, 2 * n + e]).wait()
        tot = jnp.sum(buf[slot, 2 * n] + buf[slot, 2 * n + 1], axis=-1, keepdims=True)
        res = jnp.where(lane == n, tot * (1.0 / MOBA_BLOCK), res)
    o_ref[0] = res


def _kmean_paged(page_table, cache_kt, layer):
    bs, n_pages = page_table.shape
    page = cache_kt.shape[3]
    nb = n_pages // 2
    return pl.pallas_call(
        functools.partial(_kmean_paged_kernel, layer=layer, n_pages=n_pages),
        grid_spec=pltpu.PrefetchScalarGridSpec(
            num_scalar_prefetch=1,
            grid=(bs,),
            in_specs=[pl.BlockSpec(memory_space=pl.ANY)],
            out_specs=pl.BlockSpec((1, A_WIDTH, nb), lambda b, pt: (b, 0, 0)),
            scratch_shapes=[pltpu.VMEM((2, n_pages, A_WIDTH, page), F32), pltpu.SemaphoreType.DMA((2, n_pages))],
        ),
        out_shape=jax.ShapeDtypeStruct((bs, A_WIDTH, nb), F32),
        compiler_params=_cparams(("arbitrary",)),
        name="kmean_paged",
    )(page_table, cache_kt)


def _select_kernel(q_ref, km_ref, o_ref, *, bs, nb):
    head = lax.broadcasted_iota(jnp.int32, (A_HEADS, A_WIDTH), 0)
    chan = lax.broadcasted_iota(jnp.int32, (A_HEADS, A_WIDTH), 1)
    own = (chan >= head * HEAD_DIM) & (chan < (head + 1) * HEAD_DIM)
    bid = lax.broadcasted_iota(jnp.int32, (A_HEADS, nb), 1)
    bidf = bid.astype(F32)
    lane = lax.broadcasted_iota(jnp.int32, (A_HEADS, 128), 1)

    def body(b, _):
        qm = jnp.where(own, q_ref[pl.ds(b, 1), :], 0.0)
        gate = _mxu(qm, km_ref[b], True)
        rank = jnp.zeros((A_HEADS, nb), F32)
        for m in range(nb):
            gm = gate[:, m:m + 1]
            beats = (gm > gate) | ((gm == gate) & (bid > m))
            rank = rank + jnp.where(beats, 1.0, 0.0)
        res = jnp.zeros((A_HEADS, 128), F32)
        for r in range(MOBA_TOPK):
            idx = jnp.sum(jnp.where(rank == float(r), bidf, 0.0), axis=-1, keepdims=True)
            res = jnp.where(lane == r, idx, res)
        o_ref[b] = res.astype(jnp.int32)
        return 0

    lax.fori_loop(0, bs, body, 0)


def _select(qf, kmean_t):
    bs, _, nb = kmean_t.shape
    return pl.pallas_call(
        functools.partial(_select_kernel, bs=bs, nb=nb),
        out_shape=jax.ShapeDtypeStruct((bs, A_HEADS, 128), jnp.int32),
        compiler_params=pltpu.CompilerParams(vmem_limit_bytes=VMEM_LIMIT),
        name="moba_select",
    )(qf, kmean_t)


def _decode_kernel(pt_ref, sel_ref, q_ref, kn_ref, vn_ref, ck_ref, cv_ref, o_ref, kbuf, vbuf, sem, *, layer, pg):
    t = pl.program_id(0)
    n_seq = pl.num_programs(0) - 1

    def copies(seq, slot):
        out = []
        for h in range(A_HEADS):
            for r in range(MOBA_TOPK):
                blk = sel_ref[seq, h * MOBA_TOPK + r]
                for e in range(2):
                    page = pt_ref[seq, 2 * blk + e]
                    for src, dst, s in ((ck_ref, kbuf, 0), (cv_ref, vbuf, 1)):
                        out.append(pltpu.make_async_copy(src.at[layer, page, pl.ds(h * HEAD_DIM, HEAD_DIM), :],
                                                         dst.at[slot, h, :, pl.ds((2 * r + e) * pg, pg)],
                                                         sem.at[slot, s]))
        return out

    @pl.when(t < n_seq)
    def _():
        for c in copies(t, t % 2):
            c.start()

    @pl.when(t > 0)
    def _():
        slot = (t - 1) % 2
        for c in copies(t - 1, slot):
            c.wait()
        q = q_ref[0]
        kn = kn_ref[0]
        vn = vn_ref[0]
        eye = jnp.where(lax.broadcasted_iota(jnp.int32, (HEAD_DIM, HEAD_DIM), 0)
                        == lax.broadcasted_iota(jnp.int32, (HEAD_DIM, HEAD_DIM), 1), 1.0, 0.0)
        for h in range(A_HEADS):
            qh = q[h:h + 1]
            q_col = jnp.sum(eye * qh, axis=-1, keepdims=True)
            s = jnp.sum(kbuf[slot, h] * q_col, axis=0, keepdims=True)
            s_self = jnp.sum(qh * kn[h:h + 1], axis=-1, keepdims=True)
            m = jnp.maximum(jnp.max(s, axis=-1, keepdims=True), s_self)
            p = jnp.exp2(s - m)
            p_self = jnp.exp2(s_self - m)
            l = jnp.sum(p, axis=-1, keepdims=True) + p_self
            pv_col = jnp.sum(vbuf[slot, h] * p, axis=-1, keepdims=True)
            pv = jnp.sum(eye * pv_col, axis=0, keepdims=True)
            o_ref[0, h:h + 1, :] = (pv + p_self * vn[h:h + 1]) / l


def _decode_attn(page_table, sel, qf, k_new, v_new, cache_kt, cache_vt, layer):
    bs = qf.shape[0]
    pg = cache_kt.shape[3]
    heads = lambda t: t.reshape(bs, A_HEADS, HEAD_DIM)
    row = pl.BlockSpec((1, A_HEADS, HEAD_DIM), lambda t, pt, sl: (jnp.maximum(t - 1, 0), 0, 0))
    gathered = pltpu.VMEM((2, A_HEADS, HEAD_DIM, 2 * MOBA_TOPK * pg), F32)
    out = pl.pallas_call(
        functools.partial(_decode_kernel, layer=layer, pg=pg),
        grid_spec=pltpu.PrefetchScalarGridSpec(
            num_scalar_prefetch=2,
            grid=(bs + 1,),
            in_specs=[row, row, row, pl.BlockSpec(memory_space=pl.ANY), pl.BlockSpec(memory_space=pl.ANY)],
            out_specs=row,
            scratch_shapes=[gathered, gathered, pltpu.SemaphoreType.DMA((2, 2))],
        ),
        out_shape=jax.ShapeDtypeStruct((bs, A_HEADS, HEAD_DIM), F32),
        compiler_params=_cparams(("arbitrary",)),
        name="moba_decode",
    )(page_table, sel, heads(qf), heads(k_new), heads(v_new), cache_kt, cache_vt)
    return out.reshape(bs, A_WIDTH)


def _rwkv_kernel(rkv_ref, lora_ref, prkv_ref, plora_ref, s0_ref, mu_rkv_ref, mu_lora_ref, w0_ref, wdec_ref,
                 a0_ref, wa_ref, wg_ref, kk_ref, ka_ref, rk_ref, lnw_ref, lnb_ref, g_ref, tri_ref,
                 ob_ref, s_ref, c_rkv, c_lora, s_an, s_r, s_b, s_k, s_v, s_ld, s_y, *, T, C):
    i = pl.program_id(1)

    @pl.when(i == 0)
    def _():
        c_rkv[...] = prkv_ref[0]
        c_lora[...] = plora_ref[0]
        s_ref[0] = s0_ref[0]

    x = rkv_ref[0]
    xl = lora_ref[0]
    rowi = lax.broadcasted_iota(jnp.int32, (T, 1), 0)

    def shifted(cur, prev):
        return jnp.where(rowi == 0, prev, pltpu.roll(cur, 1, 0))

    xs = shifted(x, c_rkv[...])
    xls = shifted(xl, c_lora[...])
    c_rkv[...] = x[T - 1:T]
    c_lora[...] = xl[T - 1:T]
    pm = x + (xs - x) * mu_rkv_ref[...]
    pml = xl + (xls - xl) * mu_lora_ref[...]
    r = pm[:, 0:R_WIDTH]
    k = pm[:, R_WIDTH:2 * R_WIDTH]
    v = pm[:, 2 * R_WIDTH:3 * R_WIDTH]
    t01 = pml[:, 0:128]
    dec_arg = w0_ref[...] + _dot(jnp.tanh(t01).astype(BF16), wdec_ref[...])
    sp = jnp.maximum(-dec_arg, 0.0) + jnp.log(1.0 + jnp.exp(-jnp.abs(dec_arg)))
    ld = -jnp.exp(-sp - 0.5)
    a = jax.nn.sigmoid(a0_ref[...] + _dot(t01.astype(BF16), wa_ref[...]))
    g = _dot(jax.nn.sigmoid(pml[:, 128:LORA_PAD]).astype(BF16), wg_ref[...])
    gmat = g_ref[...]

    def head_sums(t):
        return jnp.concatenate([_dot_x2(t[:, p * 128:(p + 1) * 128], gmat) for p in range(4)], axis=1)

    kk0 = k * kk_ref[...]
    kk = kk0 * lax.rsqrt(jnp.maximum(head_sums(kk0 * kk0), 1e-24))
    kmod = k * (1.0 + (a - 1.0) * ka_ref[...])
    bb = kk * a
    s_an[...] = -kk
    s_r[...] = r
    s_b[...] = bb
    s_k[...] = kmod
    s_v[...] = v
    s_ld[...] = ld

    tri = tri_ref[...]
    lane = lax.broadcasted_iota(jnp.int32, (C, 128), 1)
    in_h0 = lane < HEAD_DIM
    ri = lax.broadcasted_iota(jnp.int32, (2 * C, 2 * C), 0)
    ci = lax.broadcasted_iota(jnp.int32, (2 * C, 2 * C), 1)
    same = (ri >= C) == (ci >= C)
    strict = same & (ri > ci)
    incl = same & (ri >= ci)
    eye = jnp.where(ri == ci, 1.0, 0.0)

    def stack_masked(t):
        return jnp.concatenate([jnp.where(in_h0, t, 0.0), jnp.where(in_h0, 0.0, t)], axis=0)

    def chunk(c, _):
        rs = pl.ds(pl.multiple_of(c * C, C), C)
        ld_c = s_ld[rs, :]
        ld_hi, ld_mid, ld_lo = _split3(ld_c)
        cum = _dot(tri, ld_hi) + _dot(tri, ld_mid) + _dot(tri, ld_lo)
        cum_last = cum[C - 1:C]
        e_in = jnp.exp(cum)
        e_ex = jnp.exp(cum - ld_c)
        e_neg = jnp.exp(-cum)
        e_end = jnp.exp(cum_last - cum)
        g_end = jnp.exp(cum_last)
        at = s_an[rs, :] * e_ex
        rt = s_r[rs, :] * e_in
        bt = s_b[rs, :] * e_neg
        kt = s_k[rs, :] * e_neg
        bg = s_b[rs, :] * e_end
        kg = s_k[rs, :] * e_end
        vv = s_v[rs, :]
        pairs = range(4)
        lss = [slice(p * 128, (p + 1) * 128) for p in pairs]
        bf = lambda xs: [x.astype(BF16) for x in xs]
        ar_b = bf([jnp.concatenate([stack_masked(at[:, ls]), stack_masked(rt[:, ls])], axis=0) for ls in lss])
        bk_b = bf([jnp.concatenate([bt[:, ls], bt[:, ls], kt[:, ls], kt[:, ls]], axis=0) for ls in lss])
        vv_b = bf([stack_masked(vv[:, ls]) for ls in lss])
        bkg_b = bf([jnp.concatenate([stack_masked(bg[:, ls]), stack_masked(kg[:, ls])], axis=0) for ls in lss])
        big = [_dot_nt(a, b) for a, b in zip(ar_b, bk_b)]
        a_ab = [jnp.where(strict, m[0:2 * C, 0:2 * C], 0.0) for m in big]
        a_ak = bf([jnp.where(strict, m[0:2 * C, 2 * C:4 * C], 0.0) for m in big])
        a_rbk = bf([jnp.concatenate([jnp.where(incl, m[2 * C:4 * C, 0:2 * C], 0.0),
                                     jnp.where(incl, m[2 * C:4 * C, 2 * C:4 * C], 0.0)], axis=1) for m in big])
        inv = [eye + a for a in a_ab]
        pw = a_ab
        for _ in range(5):
            pw_b = bf(pw)
            pw = [_dot(x, x) for x in pw_b]
            inv = [iv + _dot(x.astype(BF16), iv.astype(BF16)) for x, iv in zip(pw, inv)]
        st = [s_ref[0, p] for p in pairs]
        from_state = [_dot_nt(a, s.astype(BF16)) for a, s in zip(ar_b, st)]
        rhs = [fs[0:2 * C] + _dot(a, v) for fs, a, v in zip(from_state, a_ak, vv_b)]
        u_b = bf([_dot(iv.astype(BF16), r.astype(BF16)) for iv, r in zip(inv, rhs)])
        uv_b = [jnp.concatenate([u, v], axis=0) for u, v in zip(u_b, vv_b)]
        y_s = [fs[2 * C:4 * C] + _dot(a, uv) for fs, a, uv in zip(from_state, a_rbk, uv_b)]
        for p in pairs:
            s_y[rs, lss[p]] = y_s[p][0:C] + y_s[p][C:2 * C]
            s_ref[0, p] = st[p] * g_end[:, lss[p]] + _dot_tn(uv_b[p], bkg_b[p])
        return 0

    lax.fori_loop(0, T // C, chunk, 0, unroll=True)

    y = s_y[...]
    mean = head_sums(y) * (1.0 / HEAD_DIM)
    d = y - mean
    var = head_sums(d * d) * (1.0 / HEAD_DIM)
    yn = d * lax.rsqrt(var + GN_EPS) * lnw_ref[...] + lnb_ref[...]
    bonus = head_sums(r * kmod * rk_ref[...]) * v
    ob_ref[0] = ((yn + bonus) * g).astype(BF16)


def _rwkv(rkv, lora, prev_rkv, prev_lora, s0_bd, wts):
    bsz, seq, _ = rkv.shape
    C = RWKV_CHUNK
    T = min(256, seq)
    head_of = jnp.arange(128, dtype=jnp.int32) // HEAD_DIM
    gmat = (head_of[:, None] == head_of[None, :]).astype(BF16)
    tri = (jnp.arange(C)[:, None] >= jnp.arange(C)[None, :]).astype(BF16)
    consts = list(wts) + [gmat, tri]
    seq_spec = lambda w: pl.BlockSpec((1, T, w), lambda b, i: (b, i, 0))
    per_b = lambda a: pl.BlockSpec((1,) + a.shape[1:], lambda b, i: (b,) + (0,) * (a.ndim - 1))
    full = lambda a: pl.BlockSpec(a.shape, lambda b, i: (0,) * a.ndim)
    big = lambda: pltpu.VMEM((T, R_WIDTH), F32)
    return pl.pallas_call(
        functools.partial(_rwkv_kernel, T=T, C=C),
        grid=(bsz, seq // T),
        in_specs=[seq_spec(RKV_W), seq_spec(LORA_PAD), per_b(prev_rkv), per_b(prev_lora), per_b(s0_bd)]
        + [full(a) for a in consts],
        out_specs=[seq_spec(R_WIDTH), per_b(s0_bd)],
        out_shape=[jax.ShapeDtypeStruct((bsz, seq, R_WIDTH), BF16), jax.ShapeDtypeStruct(s0_bd.shape, F32)],
        scratch_shapes=[pltpu.VMEM((1, RKV_W), F32), pltpu.VMEM((1, LORA_PAD), F32)] + [big() for _ in range(7)],
        compiler_params=_cparams(("parallel", "arbitrary")),
        name="rwkv",
    )(rkv, lora, prev_rkv, prev_lora, s0_bd, *consts)


def _rwkv_step_kernel(rkv_ref, lora_ref, prkv_ref, plora_ref, s0_ref, mu_rkv_ref, mu_lora_ref, w0_ref, wdec_ref,
                      a0_ref, wa_ref, wg_ref, kk_ref, ka_ref, rk_ref, lnw_ref, lnb_ref,
                      ob_ref, s_ref, p_r, p_w, p_k, p_v, p_kk, p_a, p_g):
    b = pl.program_id(0)

    @pl.when(b == 0)
    def _():
        x = rkv_ref[...]
        xl = lora_ref[...]
        pm = x + (prkv_ref[...] - x) * mu_rkv_ref[...]
        pml = xl + (plora_ref[...] - xl) * mu_lora_ref[...]
        k = pm[:, R_WIDTH:2 * R_WIDTH]
        t01 = pml[:, 0:128]
        dec_arg = w0_ref[...] + _mxu(jnp.tanh(t01), wdec_ref[...], True)
        sp = jnp.maximum(-dec_arg, 0.0) + jnp.log(1.0 + jnp.exp(-jnp.abs(dec_arg)))
        a = jax.nn.sigmoid(a0_ref[...] + _mxu(t01, wa_ref[...], True))
        p_r[...] = pm[:, 0:R_WIDTH]
        p_w[...] = jnp.exp(-jnp.exp(-sp - 0.5))
        p_k[...] = k * (1.0 + (a - 1.0) * ka_ref[...])
        p_v[...] = pm[:, 2 * R_WIDTH:3 * R_WIDTH]
        p_kk[...] = k * kk_ref[...]
        p_a[...] = a
        p_g[...] = _mxu(jax.nn.sigmoid(pml[:, 128:LORA_PAD]), wg_ref[...], True)
        ob_ref[...] = jnp.zeros(ob_ref.shape, F32)

    rows = pl.ds(pl.multiple_of((b // 8) * 8, 8), 8)
    mine = lax.broadcasted_iota(jnp.int32, (8, 128), 0) == b % 8
    ri = lax.broadcasted_iota(jnp.int32, (128, 128), 0)
    ci = lax.broadcasted_iota(jnp.int32, (128, 128), 1)
    same_head = (ri >= HEAD_DIM) == (ci >= HEAD_DIM)
    eye = jnp.where(ri == ci, 1.0, 0.0)
    top = lax.broadcasted_iota(jnp.int32, (128, 1), 0) < HEAD_DIM
    left = lax.broadcasted_iota(jnp.int32, (1, 128), 1) < HEAD_DIM

    def per_head_rows(t):
        s0 = jnp.sum(jnp.where(left, t, 0.0), axis=-1, keepdims=True)
        s1 = jnp.sum(jnp.where(left, 0.0, t), axis=-1, keepdims=True)
        return jnp.where(left, s0, s1)

    def per_head_cols(t):
        s0 = jnp.sum(jnp.where(top, t, 0.0), axis=0, keepdims=True)
        s1 = jnp.sum(jnp.where(top, 0.0, t), axis=0, keepdims=True)
        return jnp.where(top, s0, s1)

    for p in range(4):
        ls = slice(p * 128, (p + 1) * 128)
        r, w, k, v, kk0, a, g = (jnp.sum(jnp.where(mine, t[rows, ls], 0.0), axis=0, keepdims=True)
                                 for t in (p_r, p_w, p_k, p_v, p_kk, p_a, p_g))
        kk = kk0 * lax.rsqrt(jnp.maximum(per_head_rows(kk0 * kk0), 1e-24))
        st = s0_ref[0, p]
        sa = jnp.sum(st * kk, axis=-1, keepdims=True)
        v_col = jnp.sum(eye * v, axis=-1, keepdims=True)
        st = st * w + jnp.where(same_head, v_col * k - sa * (kk * a), 0.0)
        s_ref[0, p] = st
        y = jnp.sum(st * r, axis=-1, keepdims=True)
        d = y - per_head_cols(y) * (1.0 / HEAD_DIM)
        yn = d * lax.rsqrt(per_head_cols(d * d) * (1.0 / HEAD_DIM) + GN_EPS)
        yn_row = jnp.sum(eye * yn, axis=0, keepdims=True)
        bonus = per_head_rows(r * k * rk_ref[:, ls]) * v
        out = (yn_row * lnw_ref[:, ls] + lnb_ref[:, ls] + bonus) * g
        ob_ref[rows, ls] = jnp.where(mine, out, ob_ref[rows, ls])


def _rwkv_step(rkv, lora, prev_rkv, prev_lora, s0_bd, wts):
    bs = rkv.shape[0]
    full = lambda a: pl.BlockSpec(a.shape, lambda b: (0,) * a.ndim)
    state = pl.BlockSpec((1,) + s0_bd.shape[1:], lambda b: (b, 0, 0, 0))
    args = [rkv, lora, prev_rkv, prev_lora]
    return pl.pallas_call(
        _rwkv_step_kernel,
        grid=(bs,),
        in_specs=[full(a) for a in args] + [state] + [full(a) for a in wts],
        out_specs=[pl.BlockSpec((bs, R_WIDTH), lambda b: (0, 0)), state],
        out_shape=[jax.ShapeDtypeStruct((bs, R_WIDTH), F32), jax.ShapeDtypeStruct(s0_bd.shape, F32)],
        scratch_shapes=[pltpu.VMEM((bs, R_WIDTH), F32) for _ in range(7)],
        compiler_params=_cparams(("arbitrary",)),
        name="rwkv_step",
    )(*args, s0_bd, *wts)


def _state_to_bd(s):
    b = s.shape[0]
    s = s.reshape(b, 4, 2, HEAD_DIM, HEAD_DIM)
    z = jnp.zeros_like(s[:, :, 0])
    top = jnp.concatenate([s[:, :, 0], z], axis=-1)
    bot = jnp.concatenate([z, s[:, :, 1]], axis=-1)
    return jnp.concatenate([top, bot], axis=-2)


def _state_from_bd(s):
    b = s.shape[0]
    h0 = s[:, :, :HEAD_DIM, :HEAD_DIM]
    h1 = s[:, :, HEAD_DIM:, HEAD_DIM:]
    return jnp.stack([h0, h1], axis=2).reshape(b, 8, HEAD_DIM, HEAD_DIM)


def _merge_kernel(x_ref, gain_ref, wg_ref, oa_ref, ob_ref, wa_ref, wb_ref, wo_ref, o_ref, *, precise):
    x = x_ref[...]
    d = x.shape[1]
    gates = jax.nn.sigmoid(_mxu(_rms(x, gain_ref[...]), wg_ref[...], precise))
    ya = _mxu(oa_ref[...], wa_ref[...], precise)
    yb = _mxu(ob_ref[...], wb_ref[...], precise)
    merged = gates[:, :d] * ya + gates[:, d:] * yb
    o_ref[...] = x + _mxu(merged, wo_ref[...], precise)


def _merge(x, gain, wg, oa, ob, wa, wb, wo, precise=False):
    n, d = x.shape
    tm = min(512, n)
    row = lambda w: pl.BlockSpec((tm, w), lambda i: (i, 0))
    full = lambda a: pl.BlockSpec(a.shape, lambda i: (0, 0))
    return pl.pallas_call(
        functools.partial(_merge_kernel, precise=precise),
        grid=(n // tm,),
        in_specs=[row(d), full(gain), full(wg), row(A_WIDTH), row(R_WIDTH), full(wa), full(wb), full(wo)],
        out_specs=row(d),
        out_shape=jax.ShapeDtypeStruct((n, d), F32),
        compiler_params=_cparams(("parallel",)),
        name="merge",
    )(x, gain, wg, oa, ob, wa, wb, wo)


def _ffn_kernel(x_ref, gain_ref, w1_ref, w3_ref, w2_ref, o_ref, h_scr, acc, *, precise):
    f = pl.program_id(1)

    @pl.when(f == 0)
    def _():
        h_scr[...] = _rms(x_ref[...], gain_ref[...]).astype(h_scr.dtype)
        acc[...] = x_ref[...]

    h = h_scr[...]
    t = jax.nn.silu(_mxu(h, w1_ref[...], precise)) * _mxu(h, w3_ref[...], precise)
    acc[...] += _mxu(t, w2_ref[...], precise)

    @pl.when(f == pl.num_programs(1) - 1)
    def _():
        o_ref[...] = acc[...]


def _ffn(x, gain, w1, w3, w2, precise=False):
    n, d = x.shape
    dff = w1.shape[1]
    tm = min(512, n)
    tf = dff // 2
    return pl.pallas_call(
        functools.partial(_ffn_kernel, precise=precise),
        grid=(n // tm, dff // tf),
        in_specs=[pl.BlockSpec((tm, d), lambda i, f: (i, 0)), pl.BlockSpec((1, d), lambda i, f: (0, 0)),
                  pl.BlockSpec((d, tf), lambda i, f: (0, f)), pl.BlockSpec((d, tf), lambda i, f: (0, f)),
                  pl.BlockSpec((tf, d), lambda i, f: (f, 0))],
        out_specs=pl.BlockSpec((tm, d), lambda i, f: (i, 0)),
        out_shape=jax.ShapeDtypeStruct((n, d), F32),
        scratch_shapes=[pltpu.VMEM((tm, d), F32 if precise else BF16), pltpu.VMEM((tm, d), F32)],
        compiler_params=_cparams(("parallel", "arbitrary")),
        name="ffn",
    )(x, gain, w1, w3, w2)


def _router_kernel(x_ref, gain_ref, wr_ref, o_ref, *, n_exp):
    logits = _mxu(_rms(x_ref[...], gain_ref[...]), wr_ref[...], True)
    lane = lax.broadcasted_iota(jnp.int32, logits.shape, 1)
    lanef = lane.astype(F32)
    logits = jnp.where(lane < n_exp, logits, -jnp.inf)
    m1 = jnp.max(logits, axis=-1, keepdims=True)
    e1 = jnp.min(jnp.where(logits == m1, lanef, 1e9), axis=-1, keepdims=True)
    rest = jnp.where(lanef == e1, -jnp.inf, logits)
    m2 = jnp.max(rest, axis=-1, keepdims=True)
    e2 = jnp.min(jnp.where(rest == m2, lanef, 1e9), axis=-1, keepdims=True)
    z = jnp.exp(m2 - m1)
    g1 = 1.0 / (1.0 + z)
    g2 = z / (1.0 + z)
    res = jnp.where(lane == 0, e1, jnp.where(lane == 1, e2, jnp.where(lane == 2, g1, jnp.where(lane == 3, g2, 0.0))))
    o_ref[...] = res[:, 0:8]


def _router(x, gain, w_router):
    n, d = x.shape
    n_exp = w_router.shape[1]
    tm = min(512, n)
    wp = jnp.zeros((d, 128), F32).at[:, :n_exp].set(w_router)
    return pl.pallas_call(
        functools.partial(_router_kernel, n_exp=n_exp),
        grid=(n // tm,),
        in_specs=[pl.BlockSpec((tm, d), lambda i: (i, 0)), pl.BlockSpec((1, d), lambda i: (0, 0)),
                  pl.BlockSpec((d, 128), lambda i: (0, 0))],
        out_specs=pl.BlockSpec((tm, 8), lambda i: (i, 0)),
        out_shape=jax.ShapeDtypeStruct((n, 8), F32),
        compiler_params=_cparams(("parallel",)),
        name="router",
    )(x, gain, wp)


def _experts_kernel(be_ref, xb_ref, gain_ref, w1_ref, w3_ref, w2_ref, o_ref, *wb, precise):
    h = _rms(xb_ref[...], gain_ref[...])
    if precise:
        t = jax.nn.silu(_mxu(h, w1_ref[0], True)) * _mxu(h, w3_ref[0], True)
        o_ref[...] = _mxu(t, w2_ref[0], True)
        return
    i = pl.program_id(0)
    w1b, w3b, w2b = wb

    @pl.when((i == 0) | (be_ref[i] != be_ref[jnp.maximum(i - 1, 0)]))
    def _():
        w1b[...] = w1_ref[0].astype(BF16)
        w3b[...] = w3_ref[0].astype(BF16)
        w2b[...] = w2_ref[0].astype(BF16)

    h = h.astype(BF16)
    t = jax.nn.silu(_dot(h, w1b[...])) * _dot(h, w3b[...])
    o_ref[...] = _dot(t.astype(BF16), w2b[...])


def _experts(blk_e, xb, gain, w1, w3, w2, rb, precise):
    r, d = xb.shape
    dffe = w1.shape[2]
    wspec = lambda shape: pl.BlockSpec((1,) + shape, lambda i, be: (be[i], 0, 0))
    scratch = [] if precise else [pltpu.VMEM((d, dffe), BF16), pltpu.VMEM((d, dffe), BF16), pltpu.VMEM((dffe, d), BF16)]
    return pl.pallas_call(
        functools.partial(_experts_kernel, precise=precise),
        grid_spec=pltpu.PrefetchScalarGridSpec(
            num_scalar_prefetch=1,
            grid=(r // rb,),
            in_specs=[pl.BlockSpec((rb, d), lambda i, be: (i, 0)), pl.BlockSpec((1, d), lambda i, be: (0, 0)),
                      wspec((d, dffe)), wspec((d, dffe)), wspec((dffe, d))],
            out_specs=pl.BlockSpec((rb, d), lambda i, be: (i, 0)),
            scratch_shapes=scratch,
        ),
        out_shape=jax.ShapeDtypeStruct((r, d), F32),
        compiler_params=_cparams(("arbitrary",)),
        name="experts",
    )(blk_e, xb, gain, w1, w3, w2)


def _combine_kernel(x_ref, y0_ref, y1_ref, route_ref, *rest, final):
    g = route_ref[...]
    out = x_ref[...] + g[:, MOE_TOPK:MOE_TOPK + 1] * y0_ref[...] + g[:, MOE_TOPK + 1:MOE_TOPK + 2] * y1_ref[...]
    if final:
        gain_ref, o_ref = rest
        o_ref[...] = _rms(out, gain_ref[...])
    else:
        rest[0][...] = out


def _combine(x, y0, y1, route, final_gain):
    n, d = x.shape
    tm = min(512, n)
    row = pl.BlockSpec((tm, d), lambda i: (i, 0))
    extra = [] if final_gain is None else [final_gain]
    return pl.pallas_call(
        functools.partial(_combine_kernel, final=final_gain is not None),
        grid=(n // tm,),
        in_specs=[row, row, row, pl.BlockSpec((tm, 8), lambda i: (i, 0))] + [pl.BlockSpec((1, d), lambda i: (0, 0))] * len(extra),
        out_specs=row,
        out_shape=jax.ShapeDtypeStruct((n, d), F32),
        compiler_params=_cparams(("parallel",)),
        name="moe_combine",
    )(x, y0, y1, route, *extra)


def _moe(x, gain, w_router, w1, w3, w2, rb, precise=False, final_gain=None, anchor=None):
    n, d = x.shape
    n_exp = w_router.shape[1]
    route = _router(x, gain, w_router)
    top_e = route[:, 0:MOE_TOPK].astype(jnp.int32)
    na = n * MOE_TOPK
    e_flat = top_e.reshape(na)
    onehot = (e_flat[:, None] == jnp.arange(n_exp, dtype=jnp.int32)[None, :]).astype(jnp.int32)
    within = jnp.cumsum(onehot, axis=0) - onehot
    counts = jnp.sum(onehot, axis=0)
    padded = (counts + rb - 1) // rb * rb
    pend = jnp.cumsum(padded)
    pstart = pend - padded
    dest = pstart[e_flat] + jnp.sum(within * onehot, axis=1)
    r = (na + n_exp * (rb - 1) + rb - 1) // rb * rb
    tok = jnp.arange(na, dtype=jnp.int32) // MOE_TOPK
    row_tok = jnp.zeros((r,), jnp.int32).at[dest].set(tok, unique_indices=True)
    blk_start = jnp.arange(r // rb, dtype=jnp.int32) * rb
    blk_e = jnp.minimum(jnp.searchsorted(pend, blk_start, side="right"), n_exp - 1).astype(jnp.int32)
    if anchor is not None:
        row_tok, _ = lax.optimization_barrier((row_tok, tuple(anchor["kmeans"])))
    xb = x[row_tok]
    yield
    if anchor is not None:
        blk_e, _ = lax.optimization_barrier((blk_e, anchor["mixed"][-1]))
    yb = _experts(blk_e, xb, gain, w1, w3, w2, rb, precise)
    pos = dest.reshape(n, MOE_TOPK)
    y0, y1 = yb[pos[:, 0]], yb[pos[:, 1]]
    yield
    return _combine(x, y0, y1, route, final_gain)


def _norm_kernel(x_ref, gain_ref, o_ref):
    o_ref[...] = _rms(x_ref[...], gain_ref[...])


def _final_norm(x, gain):
    n, d = x.shape
    tm = min(1024, n)
    return pl.pallas_call(
        _norm_kernel,
        grid=(n // tm,),
        in_specs=[pl.BlockSpec((tm, d), lambda i: (i, 0)), pl.BlockSpec((1, d), lambda i: (0, 0))],
        out_specs=pl.BlockSpec((tm, d), lambda i: (i, 0)),
        out_shape=jax.ShapeDtypeStruct((n, d), F32),
        compiler_params=_cparams(("parallel",)),
        name="final_norm",
    )(x, gain)


MIXER_WEIGHTS = ("norm_attn", "w_in", "w_up_a", "mu_shift", "w0", "w_dec2", "a0", "w_a2", "w_g2", "k_k", "k_a", "r_k",
                 "ln_x_w", "ln_x_b", "w_up_b", "w_out")


def _layer_weights(l, dt, norm_attn, w_in, w_up_a, mu_shift, w0, w_dec2, a0, w_a2, w_g2, k_k, k_a, r_k, ln_x_w, ln_x_b,
                   w_up_b, w_out):
    d = w_in.shape[1]
    wi = w_in[l]
    c0 = 3 * A_WIDTH
    c1 = c0 + RKV_W
    n_lora = LORA_DECAY + LORA_ICLR + LORA_GATE
    c2 = c1 + n_lora
    wqkv = wi[:, :c0].astype(dt)
    wrkv = wi[:, c0:c1].astype(dt)
    wlora = jnp.zeros((d, LORA_PAD), dt).at[:, :n_lora].set(wi[:, c1:c2].astype(dt))
    wgate = wi[:, c2:].astype(dt)
    mu = mu_shift[l]
    mu_rkv = mu[:RKV_W][None]
    mu_lora = jnp.zeros((1, LORA_PAD), F32).at[0, :n_lora].set(mu[RKV_W:])
    wdec = jnp.zeros((128, R_WIDTH), dt).at[:LORA_DECAY].set(w_dec2[l].astype(dt))
    wa = jnp.zeros((128, R_WIDTH), dt).at[LORA_DECAY:LORA_DECAY + LORA_ICLR].set(w_a2[l].astype(dt))
    wg = jnp.zeros((LORA_PAD - 128, R_WIDTH), dt).at[:LORA_GATE].set(w_g2[l].astype(dt))
    rw = (mu_rkv, mu_lora, w0[l][None], wdec, a0[l][None], wa, wg, k_k[l][None], k_a[l][None],
          r_k[l].reshape(1, R_WIDTH), ln_x_w[l][None], ln_x_b[l][None])
    return dict(gain=norm_attn[l][None], wqkv=wqkv, wrkv=wrkv, wlora=wlora, wgate=wgate, rw=rw,
                wa=w_up_a[l].astype(dt), wb=w_up_b[l].astype(dt), wo=w_out[l].astype(dt), n_lora=n_lora)


def _mixer_prompt(x, bsz, seq, w):
    qt, kb, va, k, v, rkv, lora = _proj(x, w["gain"], w["wqkv"], w["wrkv"], w["wlora"], seq=seq)
    oa = _moba_prompt(qt, kb, va, _kmean(k), bsz, seq)
    zeros = lambda width: jnp.zeros((bsz, 1, width), F32)
    ob, s_bd = _rwkv(rkv.reshape(bsz, seq, RKV_W), lora.reshape(bsz, seq, LORA_PAD), zeros(RKV_W), zeros(LORA_PAD),
                     jnp.zeros((bsz, 4, 128, 128), F32), w["rw"])
    x = _merge(x, w["gain"], w["wgate"], oa, ob.reshape(bsz * seq, R_WIDTH), w["wa"], w["wb"], w["wo"])
    last = lambda t: t.reshape(bsz, seq, -1)[:, -1]
    p_last = jnp.concatenate([last(rkv), last(lora)[:, :w["n_lora"]]], axis=-1)
    return x, k, v, _state_from_bd(s_bd), p_last


def _mixer_sample(x, page_table, cache_kt, cache_vt, kmean_t, layer, s0, p_prev, w):
    bs = x.shape[0]
    q, k, v, rkv, lora = _proj(x, w["gain"], w["wqkv"], w["wrkv"], w["wlora"], precise=True)
    sel = _select(q, kmean_t)[:, :, :MOBA_TOPK].reshape(bs, A_HEADS * MOBA_TOPK)
    oa = _decode_attn(page_table, sel, q, k, v, cache_kt, cache_vt, layer)
    prev_lora = jnp.zeros((bs, LORA_PAD), F32).at[:, :w["n_lora"]].set(p_prev[:, RKV_W:])
    ob, s_bd = _rwkv_step(rkv, lora, p_prev[:, :RKV_W], prev_lora, _state_to_bd(s0), w["rw"])
    x = _merge(x, w["gain"], w["wgate"], oa, ob, w["wa"], w["wb"], w["wo"], precise=True)
    p_last = jnp.concatenate([rkv, lora[:, :w["n_lora"]]], axis=-1)
    return x, k, v, _state_from_bd(s_bd), p_last


def _closing_gain(wts, l):
    last = l == wts["w_in"].shape[0] - 1
    return wts["norm_final"][None] if last and l % 2 == 1 else None


def _prompt_group(x_prompt, wts, other_group):
    bp, seq, d = x_prompt.shape
    x = x_prompt.reshape(bp * seq, d)
    outs = [[] for _ in range(4)]
    for l in range(wts["w_in"].shape[0]):
        w = _layer_weights(l, BF16, *(wts[n] for n in MIXER_WEIGHTS))
        x, k, v, s, sh = _mixer_prompt(x, bp, seq, w)
        for o, t in zip(outs, (k.reshape(bp, seq, A_HEADS, HEAD_DIM), v.reshape(bp, seq, A_HEADS, HEAD_DIM), s, sh)):
            o.append(t)
        gain = wts["norm_ffn"][l][None]
        i = l // 2
        bf = lambda t: t[i].astype(BF16)
        if l % 2 == 0:
            yield
            x = _ffn(x, gain, bf(wts["ffn_w1"]), bf(wts["ffn_w3"]), bf(wts["ffn_w2"]))
            yield
        else:
            x = yield from _moe(x, gain, wts["router"][i], wts["moe_w1"][i], wts["moe_w3"][i], wts["moe_w2"][i], 256,
                                final_gain=_closing_gain(wts, l), anchor=other_group)
    y = x if _closing_gain(wts, l) is not None else _final_norm(x, wts["norm_final"][None])
    return (y.reshape(bp, seq, d),) + tuple(jnp.stack(o) for o in outs)


def _sample_group(x_sample, cache_k, cache_v, state_wkv, state_shift, page_table, wts, shared):
    bs, dec_seq, d = x_sample.shape
    depth, n_pool, page = cache_k.shape[:3]
    assert dec_seq == 1 and 2 * page == MOBA_BLOCK
    cache_kt = cache_k.transpose(0, 1, 3, 4, 2).reshape(depth, n_pool, A_WIDTH, page)
    cache_vt = cache_v.transpose(0, 1, 3, 4, 2).reshape(depth, n_pool, A_WIDTH, page)
    x = x_sample.reshape(bs, d)
    outs = [[] for _ in range(4)]
    shared["kmeans"] = [_kmean_paged(page_table, cache_kt, l) for l in range(depth)]
    shared["mixed"] = []
    for l in range(depth):
        w = _layer_weights(l, F32, *(wts[n] for n in MIXER_WEIGHTS))
        x, k, v, s, sh = _mixer_sample(x, page_table, cache_kt, cache_vt, shared["kmeans"][l], l, state_wkv[l],
                                       state_shift[l], w)
        shared["mixed"].append(x)
        for o, t in zip(outs, (k.reshape(bs, 1, A_HEADS, HEAD_DIM), v.reshape(bs, 1, A_HEADS, HEAD_DIM), s, sh)):
            o.append(t)
        gain = wts["norm_ffn"][l][None]
        i = l // 2
        if l % 2 == 0:
            yield
            x = _ffn(x, gain, wts["ffn_w1"][i], wts["ffn_w3"][i], wts["ffn_w2"][i], precise=True)
            yield
        else:
            x = yield from _moe(x, gain, wts["router"][i], wts["moe_w1"][i], wts["moe_w3"][i], wts["moe_w2"][i], 32,
                                precise=True, final_gain=_closing_gain(wts, l))
    y = x if _closing_gain(wts, l) is not None else _final_norm(x, wts["norm_final"][None])
    return (y.reshape(bs, 1, d),) + tuple(jnp.stack(o) for o in outs)


def _alternate(*stages):
    results = [None] * len(stages)
    live = list(enumerate(stages))
    while live:
        for item in list(live):
            try:
                next(item[1])
            except StopIteration as done:
                results[item[0]] = done.value
                live.remove(item)
    return results


def kernel(x_prompt, x_sample, cache_k, cache_v, state_wkv, state_shift, page_table, norm_attn, w_in, w_up_a, mu_shift, w0, w_dec2, a0, w_a2, w_g2, k_k, k_a, r_k, ln_x_w, ln_x_b, w_up_b, w_out, norm_ffn, ffn_w1, ffn_w3, ffn_w2, router, moe_w1, moe_w3, moe_w2, norm_final):
    assert x_prompt.shape[1] % MOBA_BLOCK == 0
    wts = dict(norm_attn=norm_attn, w_in=w_in, w_up_a=w_up_a, mu_shift=mu_shift, w0=w0, w_dec2=w_dec2, a0=a0, w_a2=w_a2,
               w_g2=w_g2, k_k=k_k, k_a=k_a, r_k=r_k, ln_x_w=ln_x_w, ln_x_b=ln_x_b, w_up_b=w_up_b, w_out=w_out,
               norm_ffn=norm_ffn, ffn_w1=ffn_w1, ffn_w3=ffn_w3, ffn_w2=ffn_w2, router=router, moe_w1=moe_w1,
               moe_w3=moe_w3, moe_w2=moe_w2, norm_final=norm_final)
    sample_side = {}
    (y_p, k_p, v_p, s_p, sh_p), (y_s, k_s, v_s, s_s, sh_s) = _alternate(
        _prompt_group(x_prompt, wts, sample_side),
        _sample_group(x_sample, cache_k, cache_v, state_wkv, state_shift, page_table, wts, sample_side))
    return (y_p, y_s, k_p, v_p, s_p, sh_p, k_s, v_s, s_s, sh_s)
```
